```python
import math
import jax, jax.numpy as jnp
from jax import lax
import numpy as np

D_MODEL = 2048
BATCH = 2
SEQ = 16384
DEPTH = 1

CHUNK = 64
DN_HEADS = 8
DN_DK = 128
DN_DV = 128
DN_CONV = 4
DN_QK_W = DN_HEADS * DN_DK
DN_V_W = DN_HEADS * DN_DV
DN_QKV_W = 2 * DN_QK_W + DN_V_W
SB_HEADS = 8
SB_HD = 128
SB_W = SB_HEADS * SB_HD
SB_QBLOCK = 128
IN_SIZES = (DN_QKV_W, DN_HEADS, DN_HEADS, DN_V_W, 3 * SB_W, D_MODEL, D_MODEL)
IN_WIDTH = sum(IN_SIZES)
PEER_HEADS = 8
PEER_NKEYS = 128
PEER_EXPERTS = PEER_NKEYS * PEER_NKEYS
PEER_DQ = 256
PEER_TOPK = 16
PEER_TOKBLOCK = 128
DEEPNORM_ALPHA = (2 * DEPTH) ** 0.25
DEEPNORM_BETA = (8 * DEPTH) ** -0.25
LN_EPS = 1e-5
NORM_EPS = 1e-6

kernel_name = "hybrid_deltanet_stickbreak_peer_deepnorm"


def _layernorm(x, g=None, b=None):
    xf = x.astype(jnp.float32)
    mu = jnp.mean(xf, axis=-1, keepdims=True)
    var = jnp.mean(jnp.square(xf - mu), axis=-1, keepdims=True)
    y = (xf - mu) * lax.rsqrt(var + LN_EPS)
    if g is not None:
        y = y * g.astype(jnp.float32) + b.astype(jnp.float32)
    return y.astype(x.dtype)


def _rmsnorm(x, w):
    xf = x.astype(jnp.float32)
    return xf * lax.rsqrt(jnp.mean(jnp.square(xf), -1, keepdims=True) + NORM_EPS) * w.astype(jnp.float32)


def _l2norm(x):
    return x * lax.rsqrt(jnp.sum(jnp.square(x), -1, keepdims=True) + NORM_EPS)


def _causal_dwconv(x, w):
    K = w.shape[0]
    T = x.shape[1]
    xp = jnp.pad(x, ((0, 0), (K - 1, 0), (0, 0)))
    return sum(w[i] * xp[:, i:i + T] for i in range(K))


def _heads(t, n):
    B_, T, _ = t.shape
    return t.reshape(B_, T, n, -1).transpose(0, 2, 1, 3)


def _gated_delta_rule(q, k, v, beta, g):
    B_, H, T, dk = q.shape
    dv = v.shape[-1]
    nc = T // CHUNK
    q = q.reshape(B_, H, nc, CHUNK, dk) * (dk ** -0.5)
    k = k.reshape(B_, H, nc, CHUNK, dk)
    v = v.reshape(B_, H, nc, CHUNK, dv)
    beta = beta.reshape(B_, H, nc, CHUNK)
    gc = jnp.cumsum(g.reshape(B_, H, nc, CHUNK), axis=-1)
    idx = jnp.arange(CHUNK)
    incl = idx[:, None] >= idx[None, :]
    strict = idx[:, None] > idx[None, :]
    decay = jnp.exp(jnp.where(incl, gc[..., :, None] - gc[..., None, :], -jnp.inf))
    kb = k * beta[..., None]
    L = jnp.where(strict, jnp.einsum('bhncd,bhnsd->bhncs', kb, k) * decay, 0.0)
    eye = jnp.eye(CHUNK, dtype=L.dtype)
    P = -L
    Tm = eye + P
    for _ in range(CHUNK.bit_length() - 2):
        P = P @ P
        Tm = Tm + Tm @ P
    w = Tm @ (kb * jnp.exp(gc)[..., None])
    u = Tm @ (v * beta[..., None])
    attn = jnp.einsum('bhncd,bhnsd->bhncs', q, k) * decay
    q_dec = q * jnp.exp(gc)[..., None]
    g_last = gc[..., -1]
    k_dec = k * jnp.exp(g_last[..., None] - gc)[..., None]
    chunk_decay = jnp.exp(g_last)

    def step(S, xs):
        w_c, u_c, qd_c, kd_c, a_c, cd_c = xs
        v_new = u_c - jnp.einsum('bhcd,bhde->bhce', w_c, S)
        o_c = jnp.einsum('bhcd,bhde->bhce', qd_c, S) + jnp.einsum('bhcs,bhse->bhce', a_c, v_new)
        S = S * cd_c[..., None, None] + jnp.einsum('bhcd,bhce->bhde', kd_c, v_new)
        return S, o_c

    xs = tuple(jnp.moveaxis(t, 2, 0) for t in (w, u, q_dec, k_dec, attn, chunk_decay))
    S0 = jnp.zeros((B_, H, dk, dv), jnp.float32)
    _, o = lax.scan(step, S0, xs)
    return jnp.moveaxis(o, 0, 2).reshape(B_, H, T, dv)


def _stick_breaking(q, k, v):
    T = q.shape[2]
    scale = q.shape[-1] ** -0.5
    outs = []
    for blk in range(T // SB_QBLOCK):
        s0 = blk * SB_QBLOCK
        s1 = s0 + SB_QBLOCK
        z = jnp.einsum('bhqd,bhkd->bhqk', q[:, :, s0:s1], k[:, :, :s1]) * scale
        before = jnp.arange(s1)[None, :] < jnp.arange(s0, s1)[:, None]
        log_stay = jnp.where(before, jax.nn.log_sigmoid(-z), 0.0)
        survive = lax.cumsum(log_stay, axis=3, reverse=True) - log_stay
        A = jnp.where(before, jnp.exp(jax.nn.log_sigmoid(z) + survive), 0.0)
        outs.append(jnp.einsum('bhqk,bhkd->bhqd', A, v[:, :, :s1]))
    return jnp.concatenate(outs, axis=2)


def _mixer(h, w_in, conv_w, a_log, dt_bias, dn_norm_w, w_pa, w_pb, w_o):
    B_, T, _ = h.shape
    f32 = jnp.float32
    offs = [int(o) for o in np.cumsum(IN_SIZES)[:-1]]
    p = h @ w_in
    dn_qkv, dn_b, dn_a, dn_g, sb_qkv, gate_a, gate_b = jnp.split(p, offs, axis=-1)
    qkv = jax.nn.silu(_causal_dwconv(dn_qkv, conv_w)).astype(f32)
    q, k, v = jnp.split(qkv, [DN_QK_W, 2 * DN_QK_W], axis=-1)
    q = _l2norm(_heads(q, DN_HEADS))
    k = _l2norm(_heads(k, DN_HEADS))
    v = _heads(v, DN_HEADS)
    beta = jax.nn.sigmoid(dn_b.astype(f32)).transpose(0, 2, 1)
    g = (-jnp.exp(a_log.astype(f32)) * jax.nn.softplus(dn_a.astype(f32) + dt_bias.astype(f32))).transpose(0, 2, 1)
    o_a = _gated_delta_rule(q, k, v, beta, g)
    o_a = _rmsnorm(o_a, dn_norm_w) * jax.nn.silu(_heads(dn_g, DN_HEADS).astype(f32))
    o_a = o_a.transpose(0, 2, 1, 3).reshape(B_, T, DN_V_W).astype(h.dtype)
    sq, sk, sv = jnp.split(sb_qkv.astype(f32), [SB_W, 2 * SB_W], axis=-1)
    o_b = _stick_breaking(_heads(sq, SB_HEADS), _heads(sk, SB_HEADS), _heads(sv, SB_HEADS))
    o_b = o_b.transpose(0, 2, 1, 3).reshape(B_, T, SB_W).astype(h.dtype)
    merged = jax.nn.sigmoid(gate_a) * (o_a @ w_pa) + jax.nn.sigmoid(gate_b) * (o_b @ w_pb)
    return merged @ w_o


def _peer(h, w_q, sub_keys, u_tab, v_tab):
    B_, T, D = h.shape
    hb = h.reshape((B_ * T) // PEER_TOKBLOCK, PEER_TOKBLOCK, D)

    def block(xb):
        n = xb.shape[0]
        q = (xb @ w_q).reshape(n, PEER_HEADS, 2, PEER_DQ // 2)
        s = jnp.einsum('nhpd,hpkd->nhpk', q, sub_keys).astype(jnp.float32)
        sv, si = lax.top_k(s, PEER_TOPK)
        cand = (sv[:, :, 0, :, None] + sv[:, :, 1, None, :]).reshape(n, PEER_HEADS, PEER_TOPK * PEER_TOPK)
        cidx = (si[:, :, 0, :, None] * PEER_NKEYS + si[:, :, 1, None, :]).reshape(n, PEER_HEADS, PEER_TOPK * PEER_TOPK)
        best, pos = lax.top_k(cand, PEER_TOPK)
        eidx = jnp.take_along_axis(cidx, pos, axis=-1)
        gate = jax.nn.softmax(best, axis=-1)
        u_sel = u_tab[eidx]
        act = jax.nn.gelu(jnp.einsum('nd,nhkd->nhk', xb, u_sel).astype(jnp.float32), approximate=False)
        v_sel = v_tab[eidx]
        return jnp.einsum('nhk,nhkd->nd', (gate * act).astype(xb.dtype), v_sel)

    return lax.map(block, hb).reshape(B_, T, D)


def setup_inputs(seed: int = 0) -> dict:
    key = jax.random.key(seed)
    ks = jax.random.split(key, 24)
    f32 = jnp.float32

    def nrm(k, shape, s):
        return jax.random.normal(k, shape, f32) * s

    x = nrm(ks[0], (BATCH, SEQ, D_MODEL), 1.0)
    c = nrm(ks[1], (BATCH, D_MODEL), 1.0)
    w_ada = nrm(ks[2], (DEPTH, D_MODEL, 6 * D_MODEL), 0.2 * D_MODEL ** -0.5)
    b_ada = nrm(ks[3], (DEPTH, 6 * D_MODEL), 0.01)
    w_in = nrm(ks[4], (DEPTH, D_MODEL, IN_WIDTH), D_MODEL ** -0.5)
    conv_w = nrm(ks[5], (DEPTH, DN_CONV, DN_QKV_W), DN_CONV ** -0.5)
    a_log = jnp.log(jax.random.uniform(ks[6], (DEPTH, DN_HEADS), f32, 1.0, 16.0))
    dt = jnp.exp(jax.random.uniform(ks[7], (DEPTH, DN_HEADS), f32, math.log(1e-3), math.log(1e-1)))
    dt_bias = dt + jnp.log(-jnp.expm1(-dt))
    dn_norm_w = 1.0 + nrm(ks[8], (DEPTH, DN_DV), 0.01)
    w_pa = nrm(ks[9], (DEPTH, DN_V_W, D_MODEL), DN_V_W ** -0.5)
    w_pb = nrm(ks[10], (DEPTH, SB_W, D_MODEL), SB_W ** -0.5)
    w_o = nrm(ks[11], (DEPTH, D_MODEL, D_MODEL), DEEPNORM_BETA * D_MODEL ** -0.5)
    ln1_g = 1.0 + nrm(ks[12], (DEPTH, D_MODEL), 0.01)
    ln1_b = nrm(ks[13], (DEPTH, D_MODEL), 0.01)
    peer_wq = nrm(ks[14], (DEPTH, D_MODEL, PEER_HEADS * PEER_DQ), D_MODEL ** -0.5)
    peer_keys = nrm(ks[15], (DEPTH, PEER_HEADS, 2, PEER_NKEYS, PEER_DQ // 2), (PEER_DQ // 2) ** -0.5)
    peer_u = nrm(ks[16], (DEPTH, PEER_EXPERTS, D_MODEL), D_MODEL ** -0.5)
    peer_v = nrm(ks[17], (DEPTH, PEER_EXPERTS, D_MODEL), DEEPNORM_BETA * PEER_HEADS ** -0.5)
    ln2_g = 1.0 + nrm(ks[18], (DEPTH, D_MODEL), 0.01)
    ln2_b = nrm(ks[19], (DEPTH, D_MODEL), 0.01)
    return {"x": x, "c": c, "w_ada": w_ada, "b_ada": b_ada, "w_in": w_in, "conv_w": conv_w,
            "a_log": a_log, "dt_bias": dt_bias, "dn_norm_w": dn_norm_w, "w_pa": w_pa, "w_pb": w_pb,
            "w_o": w_o, "ln1_g": ln1_g, "ln1_b": ln1_b, "peer_wq": peer_wq, "peer_keys": peer_keys,
            "peer_u": peer_u, "peer_v": peer_v, "ln2_g": ln2_g, "ln2_b": ln2_b}


def reference(x, c, w_ada, b_ada, w_in, conv_w, a_log, dt_bias, dn_norm_w, w_pa, w_pb, w_o,
              ln1_g, ln1_b, peer_wq, peer_keys, peer_u, peer_v, ln2_g, ln2_b):
    for l in range(DEPTH):
        mod = jax.nn.silu(c) @ w_ada[l] + b_ada[l]
        sh1, sc1, g1, sh2, sc2, g2 = jnp.split(mod[:, None, :], 6, axis=-1)
        h = _layernorm(x) * (1.0 + sc1) + sh1
        y = _mixer(h, w_in[l], conv_w[l], a_log[l], dt_bias[l], dn_norm_w[l], w_pa[l], w_pb[l], w_o[l])
        x = _layernorm(DEEPNORM_ALPHA * x + (1.0 + g1) * y, ln1_g[l], ln1_b[l])
        h = _layernorm(x) * (1.0 + sc2) + sh2
        y = _peer(h, peer_wq[l], peer_keys[l], peer_u[l], peer_v[l])
        x = _layernorm(DEEPNORM_ALPHA * x + (1.0 + g2) * y, ln2_g[l], ln2_b[l])
    return x
```

```python
import functools
import math

import jax
import jax.numpy as jnp
from jax import lax
from jax.experimental import pallas as pl
from jax.experimental.pallas import tpu as pltpu

F32 = jnp.float32
BF16 = jnp.bfloat16
HIGHEST = lax.Precision.HIGHEST

LN_EPS = 1e-5
NORM_EPS = 1e-6
D_MODEL = 2048
LANES = 128
CHUNK = 64
N_HEADS = 8
HEAD_DIM = 128
DN_CONV = 4
QKV_W = 3 * N_HEADS * HEAD_DIM
SB_QBLOCK = 128
PEER_NKEYS = 128
PEER_TOPK = 16
EXP_UNDERFLOW = -104.0
VMEM_LIMIT = 56 * 1024 * 1024

COL_DNQKV = 0
COL_SBQKV = 3072
COL_GATE_A = 6144
COL_GATE_B = 8192
COL_DNG = 10240
MAIN_W = 11264


def _cparams(*sem):
    return pltpu.CompilerParams(dimension_semantics=sem, vmem_limit_bytes=VMEM_LIMIT)


def _dot(a, b):
    return jnp.dot(a.astype(BF16), b.astype(BF16), preferred_element_type=F32)


def _dot_nt(a, b):
    return lax.dot_general(a.astype(BF16), b.astype(BF16), (((1,), (1,)), ((), ())),
                           preferred_element_type=F32)


def _dot_tn(a, b):
    return lax.dot_general(a.astype(BF16), b.astype(BF16), (((0,), (0,)), ((), ())),
                           preferred_element_type=F32)


def _silu(x):
    return x * jax.nn.sigmoid(x)


def _softplus(x):
    return jnp.maximum(x, 0.0) + jnp.log1p(jnp.exp(-jnp.abs(x)))


def _layernorm(x):
    mu = jnp.mean(x, axis=-1, keepdims=True)
    xc = x - mu
    var = jnp.mean(xc * xc, axis=-1, keepdims=True)
    return xc * lax.rsqrt(var + LN_EPS)


def _ada_kernel(c_ref, w_ref, b_ref, o_ref):
    s = _silu(c_ref[...])
    o_ref[...] = jnp.dot(s, w_ref[...], preferred_element_type=F32, precision=HIGHEST) + b_ref[...]


def _ada(c, w_ada, b_ada):
    bsz = c.shape[0]
    width = w_ada.shape[1]
    tn = 1024
    cp = jnp.zeros((8, D_MODEL), F32).at[:bsz].set(c)
    out = pl.pallas_call(
        _ada_kernel,
        grid=(width // tn,),
        in_specs=[pl.BlockSpec((8, D_MODEL), lambda j: (0, 0)),
                  pl.BlockSpec((D_MODEL, tn), lambda j: (0, j)),
                  pl.BlockSpec((1, tn), lambda j: (0, j))],
        out_specs=pl.BlockSpec((8, tn), lambda j: (0, j)),
        out_shape=jax.ShapeDtypeStruct((8, width), F32),
        compiler_params=_cparams("parallel"),
        name="ada",
    )(cp, w_ada, b_ada.reshape(1, width))
    return out[:bsz]


def _inproj_kernel(x_ref, scp_ref, sh_ref, w_ref, wba_ref, p_ref, pba_ref, h_scr):
    @pl.when(pl.program_id(1) == 0)
    def _():
        h = (_layernorm(x_ref[...]) * scp_ref[0] + sh_ref[0]).astype(BF16)
        h_scr[...] = h
        pba_ref[...] = jnp.dot(h, wba_ref[...], preferred_element_type=F32)

    p_ref[...] = jnp.dot(h_scr[...], w_ref[...], preferred_element_type=F32).astype(p_ref.dtype)


def _inproj(x2, scp, sh, w_main, w_ba, seq):
    n = x2.shape[0]
    tm = min(1024, seq)
    tn = 1024
    tpb = seq // tm
    return pl.pallas_call(
        _inproj_kernel,
        grid=(n // tm, MAIN_W // tn),
        in_specs=[pl.BlockSpec((tm, D_MODEL), lambda i, j: (i, 0)),
                  pl.BlockSpec((1, 1, D_MODEL), lambda i, j: (i // tpb, 0, 0)),
                  pl.BlockSpec((1, 1, D_MODEL), lambda i, j: (i // tpb, 0, 0)),
                  pl.BlockSpec((D_MODEL, tn), lambda i, j: (0, j)),
                  pl.BlockSpec((D_MODEL, LANES), lambda i, j: (0, 0))],
        out_specs=[pl.BlockSpec((tm, tn), lambda i, j: (i, j)),
                   pl.BlockSpec((tm, LANES), lambda i, j: (i, 0))],
        out_shape=[jax.ShapeDtypeStruct((n, MAIN_W), BF16),
                   jax.ShapeDtypeStruct((n, LANES), F32)],
        scratch_shapes=[pltpu.VMEM((tm, D_MODEL), BF16)],
        compiler_params=_cparams("parallel", "arbitrary"),
        name="inproj",
    )(x2, scp, sh, w_main, w_ba)


def _conv_kernel(x_ref, w_ref, o_ref, prev_scr):
    tt = x_ref.shape[0]

    @pl.when(pl.program_id(1) == 0)
    def _():
        prev_scr[...] = jnp.zeros_like(prev_scr)

    for g in range(QKV_W // LANES):
        cols = slice(g * LANES, (g + 1) * LANES)
        x = x_ref[:, cols].astype(F32)
        xp = jnp.concatenate([prev_scr[:, cols], x], axis=0)
        w = w_ref[:, cols]
        y = w[DN_CONV - 1:DN_CONV] * x
        for i in range(DN_CONV - 1):
            off = 8 - (DN_CONV - 1) + i
            y = y + w[i:i + 1] * xp[off:off + tt]
        prev_scr[:, cols] = x[tt - 8:]
        y = _silu(y)
        if g < 2 * N_HEADS:
            y = y * lax.rsqrt(jnp.sum(y * y, axis=-1, keepdims=True) + NORM_EPS)
        o_ref[:, cols] = y.astype(o_ref.dtype)


def _conv(p, conv_w, bsz, seq):
    tt = min(256, seq)
    tpb = seq // tt
    return pl.pallas_call(
        _conv_kernel,
        grid=(bsz, tpb),
        in_specs=[pl.BlockSpec((tt, QKV_W), lambda b, t: (b * tpb + t, COL_DNQKV // QKV_W)),
                  pl.BlockSpec((DN_CONV, QKV_W), lambda b, t: (0, 0))],
        out_specs=pl.BlockSpec((tt, QKV_W), lambda b, t: (b * tpb + t, 0)),
        out_shape=jax.ShapeDtypeStruct((bsz * seq, QKV_W), BF16),
        scratch_shapes=[pltpu.VMEM((8, QKV_W), F32)],
        compiler_params=_cparams("parallel", "arbitrary"),
        name="conv",
    )(p, conv_w)


def _delta_kernel(q_ref, k_ref, v_ref, gate_ref, ba_ref, avec_ref, dtvec_ref, nw_ref, o_ref,
                  s_scr, beta_scr, g_scr):
    tt = q_ref.shape[0]
    h = pl.program_id(1)

    @pl.when(pl.program_id(2) == 0)
    def _():
        s_scr[...] = jnp.zeros_like(s_scr)

    ba = ba_ref[...]
    lane = lax.broadcasted_iota(jnp.int32, (tt, LANES), 1)
    b_col = jnp.sum(jnp.where(lane == h, ba, 0.0), axis=1, keepdims=True)
    g_all = -jnp.exp(avec_ref[...]) * _softplus(ba + dtvec_ref[...])
    g_col = jnp.sum(jnp.where(lane == h + N_HEADS, g_all, 0.0), axis=1, keepdims=True)
    beta_scr[...] = jnp.broadcast_to(jax.nn.sigmoid(b_col), (tt, LANES))
    g_scr[...] = jnp.broadcast_to(g_col, (tt, LANES))

    r = lax.broadcasted_iota(jnp.int32, (CHUNK, CHUNK), 0)
    c = lax.broadcasted_iota(jnp.int32, (CHUNK, CHUNK), 1)
    incl = r >= c
    strict = r > c
    lincl = incl.astype(F32)
    strictf = strict.astype(F32)
    eye = (r == c).astype(F32)

    def chunk(ci, carry):
        rows = pl.ds(pl.multiple_of(ci * CHUNK, CHUNK), CHUNK)
        q = q_ref[rows, :].astype(F32) * (HEAD_DIM ** -0.5)
        k = k_ref[rows, :].astype(F32)
        v = v_ref[rows, :].astype(F32)
        beta = beta_scr[rows, :]
        gb = g_scr[rows, :]
        gc = jnp.dot(lincl, gb, preferred_element_type=F32, precision=HIGHEST)
        dm = jnp.dot(lincl, gb[:, :CHUNK] * strictf, preferred_element_type=F32, precision=HIGHEST)
        decay = jnp.where(incl, jnp.exp(dm), 0.0)
        egc = jnp.exp(gc)
        gl = gc[CHUNK - 1:CHUNK, :]
        kb = k * beta
        p = jnp.where(strict, -(_dot_nt(kb, k) * decay), 0.0)
        tm = eye + p
        for _ in range(CHUNK.bit_length() - 2):
            p = _dot(p, p)
            tm = tm + _dot(tm, p)
        w = _dot(tm, kb * egc)
        u = _dot(tm, v * beta)
        attn = _dot_nt(q, k) * decay
        s = s_scr[...]
        v_new = u - _dot(w, s)
        o = _dot(q * egc, s) + _dot(attn, v_new)
        kd = k * jnp.exp(gl - gc)
        s_scr[...] = s * jnp.exp(gl) + _dot_tn(kd, v_new)
        on = o * lax.rsqrt(jnp.mean(o * o, axis=-1, keepdims=True) + NORM_EPS) * nw_ref[...]
        o_ref[rows, :] = (on * _silu(gate_ref[rows, :].astype(F32))).astype(o_ref.dtype)
        return carry

    lax.fori_loop(0, tt // CHUNK, chunk, 0)


def _delta(qkv, p, p_ba, avec, dtvec, nw, bsz, seq):
    tt = min(1024, seq)
    tpb = seq // tt
    row = lambda b, h, t: b * tpb + t
    return pl.pallas_call(
        _delta_kernel,
        grid=(bsz, N_HEADS, tpb),
        in_specs=[pl.BlockSpec((tt, HEAD_DIM), lambda b, h, t: (row(b, h, t), h)),
                  pl.BlockSpec((tt, HEAD_DIM), lambda b, h, t: (row(b, h, t), N_HEADS + h)),
                  pl.BlockSpec((tt, HEAD_DIM), lambda b, h, t: (row(b, h, t), 2 * N_HEADS + h)),
                  pl.BlockSpec((tt, HEAD_DIM), lambda b, h, t: (row(b, h, t), COL_DNG // HEAD_DIM + h)),
                  pl.BlockSpec((tt, LANES), lambda b, h, t: (row(b, h, t), 0)),
                  pl.BlockSpec((1, LANES), lambda b, h, t: (0, 0)),
                  pl.BlockSpec((1, LANES), lambda b, h, t: (0, 0)),
                  pl.BlockSpec((1, HEAD_DIM), lambda b, h, t: (0, 0))],
        out_specs=pl.BlockSpec((tt, HEAD_DIM), lambda b, h, t: (row(b, h, t), h)),
        out_shape=jax.ShapeDtypeStruct((bsz * seq, N_HEADS * HEAD_DIM), BF16),
        scratch_shapes=[pltpu.VMEM((HEAD_DIM, HEAD_DIM), F32),
                        pltpu.VMEM((tt, LANES), F32),
                        pltpu.VMEM((tt, LANES), F32)],
        compiler_params=_cparams("parallel", "parallel", "arbitrary"),
        name="delta",
    )(qkv, qkv, qkv, p, p_ba, avec, dtvec, nw)


def _sb_kernel(q_ref, k_ref, v_ref, o_ref):
    qi = pl.program_id(2)
    blk = SB_QBLOCK
    q = q_ref[...]
    scale = HEAD_DIM ** -0.5
    row = lax.broadcasted_iota(jnp.int32, (blk, blk), 0)
    col = lax.broadcasted_iota(jnp.int32, (blk, blk), 1)
    below = col < row
    r2 = lax.broadcasted_iota(jnp.int32, (blk, 2 * blk), 0)
    c2 = lax.broadcasted_iota(jnp.int32, (blk, 2 * blk), 1)
    cum_m = jnp.where((r2 > c2) | (c2 >= blk), 1.0, 0.0).astype(BF16)

    def cond(carry):
        kb, _, run = carry
        return jnp.logical_and(kb >= 0, jnp.max(run) >= EXP_UNDERFLOW)

    def body(carry):
        kb, acc, run = carry
        ks = pl.ds(pl.multiple_of(kb * blk, blk), blk)
        z = _dot_nt(q, k_ref[ks, :]) * scale
        sp = _softplus(z)
        valid = jnp.logical_or(below, kb < qi)
        ls = jnp.where(valid, -sp, 0.0)
        hi = ls.astype(BF16)
        lo = (ls - hi.astype(F32)).astype(BF16)
        ct = (jnp.dot(hi, cum_m, preferred_element_type=F32)
              + jnp.dot(lo, cum_m, preferred_element_type=F32))
        log_a = (z - sp) + ct[:, :blk] + run
        a = jnp.where(valid, jnp.exp(log_a), 0.0)
        acc = acc + _dot(a, v_ref[ks, :])
        return kb - 1, acc, run + ct[:, blk:]

    zeros = jnp.zeros((blk, blk), F32)
    _, acc, _ = lax.while_loop(cond, body, (qi, zeros, zeros))
    o_ref[...] = acc.astype(o_ref.dtype)


def _sb(p, bsz, seq):
    nq = seq // SB_QBLOCK
    cq = COL_SBQKV // HEAD_DIM
    return pl.pallas_call(
        _sb_kernel,
        grid=(bsz, N_HEADS, nq),
        in_specs=[pl.BlockSpec((SB_QBLOCK, HEAD_DIM), lambda b, h, i: (b * nq + i, cq + h)),
                  pl.BlockSpec((seq, HEAD_DIM), lambda b, h, i: (b, cq + N_HEADS + h)),
                  pl.BlockSpec((seq, HEAD_DIM), lambda b, h, i: (b, cq + 2 * N_HEADS + h))],
        out_specs=pl.BlockSpec((SB_QBLOCK, HEAD_DIM), lambda b, h, i: (b * nq + i, h)),
        out_shape=jax.ShapeDtypeStruct((bsz * seq, N_HEADS * HEAD_DIM), BF16),
        compiler_params=_cparams("parallel", "parallel", "arbitrary"),
        name="sb",
    )(p, p, p)


def _merge_kernel(oa_ref, ob_ref, ga_ref, gb_ref, wpa_ref, wpb_ref, o_ref):
    ya = jnp.dot(oa_ref[...], wpa_ref[...], preferred_element_type=F32)
    yb = jnp.dot(ob_ref[...], wpb_ref[...], preferred_element_type=F32)
    m = (jax.nn.sigmoid(ga_ref[...].astype(F32)) * ya
         + jax.nn.sigmoid(gb_ref[...].astype(F32)) * yb)
    o_ref[...] = m.astype(o_ref.dtype)


def _merge(o_a, o_b, p, w_pa, w_pb):
    n = o_a.shape[0]
    tm = 512
    hw = N_HEADS * HEAD_DIM
    return pl.pallas_call(
        _merge_kernel,
        grid=(n // tm,),
        in_specs=[pl.BlockSpec((tm, hw), lambda i: (i, 0)),
                  pl.BlockSpec((tm, hw), lambda i: (i, 0)),
                  pl.BlockSpec((tm, D_MODEL), lambda i: (i, COL_GATE_A // D_MODEL)),
                  pl.BlockSpec((tm, D_MODEL), lambda i: (i, COL_GATE_B // D_MODEL)),
                  pl.BlockSpec((hw, D_MODEL), lambda i: (0, 0)),
                  pl.BlockSpec((hw, D_MODEL), lambda i: (0, 0))],
        out_specs=pl.BlockSpec((tm, D_MODEL), lambda i: (i, 0)),
        out_shape=jax.ShapeDtypeStruct((n, D_MODEL), BF16),
        compiler_params=_cparams("parallel"),
        name="merge",
    )(o_a, o_b, p, p, w_pa, w_pb)


def _post1_kernel(alpha, m_ref, x_ref, wo_ref, g1p_ref, lg_ref, lb_ref, scp_ref, sh_ref,
                  x1_ref, ht_ref):
    y = jnp.dot(m_ref[...], wo_ref[...], preferred_element_type=F32)
    x1 = _layernorm(alpha * x_ref[...] + g1p_ref[0] * y) * lg_ref[...] + lb_ref[...]
    x1_ref[...] = x1
    h2 = _layernorm(x1) * scp_ref[0] + sh_ref[0]
    ht_ref[...] = h2.T.astype(ht_ref.dtype)


def _post1(merged, x2, w_o, g1p, ln_g, ln_b, scp2, sh2, seq, alpha):
    n = x2.shape[0]
    tm = min(512, seq)
    tpb = seq // tm
    bvec = lambda i: (i // tpb, 0, 0)
    return pl.pallas_call(
        functools.partial(_post1_kernel, alpha),
        grid=(n // tm,),
        in_specs=[pl.BlockSpec((tm, D_MODEL), lambda i: (i, 0)),
                  pl.BlockSpec((tm, D_MODEL), lambda i: (i, 0)),
                  pl.BlockSpec((D_MODEL, D_MODEL), lambda i: (0, 0)),
                  pl.BlockSpec((1, 1, D_MODEL), bvec),
                  pl.BlockSpec((1, D_MODEL), lambda i: (0, 0)),
                  pl.BlockSpec((1, D_MODEL), lambda i: (0, 0)),
                  pl.BlockSpec((1, 1, D_MODEL), bvec),
                  pl.BlockSpec((1, 1, D_MODEL), bvec)],
        out_specs=[pl.BlockSpec((tm, D_MODEL), lambda i: (i, 0)),
                   pl.BlockSpec((D_MODEL, tm), lambda i: (0, i))],
        out_shape=[jax.ShapeDtypeStruct((n, D_MODEL), F32),
                   jax.ShapeDtypeStruct((D_MODEL, n), BF16)],
        compiler_params=_cparams("parallel"),
        name="post1",
    )(merged, x2, w_o, g1p, ln_g, ln_b, scp2, sh2)


def _extract_top(s, count, out_scr):
    nrows = s.shape[0]
    rid = lax.broadcasted_iota(jnp.int32, s.shape, 0).astype(F32)
    m = None
    for r in range(count):
        m = jnp.max(s, axis=0, keepdims=True)
        first = jnp.min(jnp.where(s == m, rid, float(nrows)), axis=0, keepdims=True)
        s = jnp.where(rid == first, -jnp.inf, s)
        if out_scr is not None:
            out_scr[r:r + 1, :] = m
    return m


def _route_kernel(ht_ref, wqt_ref, keys_ref, s1_ref, e1_ref, s2_ref, e2_ref, tau_ref,
                  st_scr, v1_scr, v2_scr, cand_scr):
    qt = jnp.dot(wqt_ref[...], ht_ref[...], preferred_element_type=F32)
    for hp in range(2 * N_HEADS):
        st_scr[hp] = jnp.dot(keys_ref[hp], qt[hp * HEAD_DIM:(hp + 1) * HEAD_DIM].astype(BF16),
                             preferred_element_type=F32)
    rid8 = lax.broadcasted_iota(jnp.int32, (8, LANES), 0)
    neg = -jnp.inf

    def head(h, carry):
        s1 = st_scr[2 * h]
        s2 = st_scr[2 * h + 1]
        _extract_top(s1, PEER_TOPK, v1_scr)
        _extract_top(s2, PEER_TOPK, v2_scr)
        cand_scr[0:16, :] = v1_scr[0:1, :] + v2_scr[0:16, :]
        lo8 = v2_scr[0:8, :]
        for a in range(1, 8):
            nb = PEER_TOPK // (a + 1)
            cand_scr[8 + 8 * a:16 + 8 * a, :] = jnp.where(rid8 < nb, v1_scr[a:a + 1, :] + lo8, neg)
        cand_scr[72:80, :] = v1_scr[8:16, :] + v2_scr[0:1, :]
        cand = cand_scr[...]
        tau = _extract_top(cand, PEER_TOPK, None)
        top = v1_scr[0:1, :] + v2_scr[0:1, :]
        z = jnp.sum(jnp.where(cand >= tau, jnp.exp(cand - top), 0.0), axis=0, keepdims=True)
        s1_ref[h] = s1
        s2_ref[h] = s2
        e1_ref[h] = jnp.exp(s1 - v1_scr[0:1, :])
        e2_ref[h] = jnp.exp(s2 - v2_scr[0:1, :]) * (1.0 / z)
        tau_ref[pl.ds(h, 1), :] = tau
        return carry

    lax.fori_loop(0, N_HEADS, head, 0)


def _route(ht, wqt, keys):
    n = ht.shape[1]
    tn = LANES
    tab = jax.ShapeDtypeStruct((N_HEADS, PEER_NKEYS, n), F32)
    tab_spec = pl.BlockSpec((N_HEADS, PEER_NKEYS, tn), lambda t: (0, 0, t))
    return pl.pallas_call(
        _route_kernel,
        grid=(n // tn,),
        in_specs=[pl.BlockSpec((D_MODEL, tn), lambda t: (0, t)),
                  pl.BlockSpec((D_MODEL, D_MODEL), lambda t: (0, 0)),
                  pl.BlockSpec((2 * N_HEADS, PEER_NKEYS, HEAD_DIM), lambda t: (0, 0, 0))],
        out_specs=[tab_spec, tab_spec, tab_spec, tab_spec,
                   pl.BlockSpec((N_HEADS, tn), lambda t: (0, t))],
        out_shape=[tab, tab, tab, tab, jax.ShapeDtypeStruct((N_HEADS, n), F32)],
        scratch_shapes=[pltpu.VMEM((2 * N_HEADS, PEER_NKEYS, tn), F32),
                        pltpu.VMEM((PEER_TOPK, tn), F32),
                        pltpu.VMEM((PEER_TOPK, tn), F32),
                        pltpu.VMEM((80, tn), F32)],
        compiler_params=_cparams("parallel"),
        name="route",
    )(ht, wqt, keys)


EXPERTS_PER_STEP = 1024


def _gelu(x):
    return 0.5 * x * (1.0 + lax.erf(x * (2.0 ** -0.5)))


def _dense_kernel(ht_ref, u_ref, vt_ref, s1_ref, e1_ref, s2_ref, e2_ref, tau_ref, yt_ref,
                  act_scr, wa_scr):
    tn = ht_ref.shape[1]

    @pl.when(pl.program_id(1) == 0)
    def _():
        yt_ref[...] = jnp.zeros_like(yt_ref)

    for ii in range(EXPERTS_PER_STEP // PEER_NKEYS):
        rows = slice(ii * PEER_NKEYS, (ii + 1) * PEER_NKEYS)
        act_scr[...] = _gelu(jnp.dot(u_ref[rows, :], ht_ref[...], preferred_element_type=F32))
        for lt in range(tn // LANES):
            cols = slice(lt * LANES, (lt + 1) * LANES)
            w = jnp.zeros((PEER_NKEYS, LANES), F32)
            for h in range(N_HEADS):
                total = s1_ref[h, ii:ii + 1, cols] + s2_ref[h, :, cols]
                gate = e1_ref[h, ii:ii + 1, cols] * e2_ref[h, :, cols]
                w = w + jnp.where(total >= tau_ref[h:h + 1, cols], gate, 0.0)
            wa_scr[rows, cols] = (w * act_scr[:, cols]).astype(BF16)
    yt_ref[...] += jnp.dot(vt_ref[...], wa_scr[...], preferred_element_type=F32)


def _dense(ht, u_tab, vt_tab, s1, e1, s2, e2, tau):
    n = ht.shape[1]
    tn = 512
    n_exp = u_tab.shape[0]
    rows_per_step = EXPERTS_PER_STEP // PEER_NKEYS
    row_spec = pl.BlockSpec((N_HEADS, rows_per_step, tn), lambda t, e: (0, e, t))
    tab_spec = pl.BlockSpec((N_HEADS, PEER_NKEYS, tn), lambda t, e: (0, 0, t))
    return pl.pallas_call(
        _dense_kernel,
        grid=(n // tn, n_exp // EXPERTS_PER_STEP),
        in_specs=[pl.BlockSpec((D_MODEL, tn), lambda t, e: (0, t)),
                  pl.BlockSpec((EXPERTS_PER_STEP, D_MODEL), lambda t, e: (e, 0)),
                  pl.BlockSpec((D_MODEL, EXPERTS_PER_STEP), lambda t, e: (0, e)),
                  row_spec, row_spec, tab_spec, tab_spec,
                  pl.BlockSpec((N_HEADS, tn), lambda t, e: (0, t))],
        out_specs=pl.BlockSpec((D_MODEL, tn), lambda t, e: (0, t)),
        out_shape=jax.ShapeDtypeStruct((D_MODEL, n), F32),
        scratch_shapes=[pltpu.VMEM((PEER_NKEYS, tn), F32),
                        pltpu.VMEM((EXPERTS_PER_STEP, tn), BF16)],
        compiler_params=_cparams("parallel", "arbitrary"),
        name="dense",
    )(ht, u_tab, vt_tab, s1, e1, s2, e2, tau)


def _post2_kernel(alpha, yt_ref, x_ref, g2p_ref, lg_ref, lb_ref, o_ref):
    y = yt_ref[...].T
    o_ref[...] = _layernorm(alpha * x_ref[...] + g2p_ref[0] * y) * lg_ref[...] + lb_ref[...]


def _post2(yt, x1, g2p, ln_g, ln_b, seq, alpha):
    n = x1.shape[0]
    tm = min(512, seq)
    tpb = seq // tm
    return pl.pallas_call(
        functools.partial(_post2_kernel, alpha),
        grid=(n // tm,),
        in_specs=[pl.BlockSpec((D_MODEL, tm), lambda i: (0, i)),
                  pl.BlockSpec((tm, D_MODEL), lambda i: (i, 0)),
                  pl.BlockSpec((1, 1, D_MODEL), lambda i: (i // tpb, 0, 0)),
                  pl.BlockSpec((1, D_MODEL), lambda i: (0, 0)),
                  pl.BlockSpec((1, D_MODEL), lambda i: (0, 0))],
        out_specs=pl.BlockSpec((tm, D_MODEL), lambda i: (i, 0)),
        out_shape=jax.ShapeDtypeStruct((n, D_MODEL), F32),
        compiler_params=_cparams("parallel"),
        name="post2",
    )(yt, x1, g2p, ln_g, ln_b)


def _layer(x2, c, bsz, seq, alpha, w_ada, b_ada, w_in, conv_w, a_log, dt_bias, dn_norm_w, w_pa,
           w_pb, w_o, ln1_g, ln1_b, peer_wq, peer_keys, peer_u, peer_v, ln2_g, ln2_b):
    mod = _ada(c, w_ada, b_ada).reshape(bsz, 6, 1, D_MODEL)
    sh1, sc1, g1, sh2, sc2, g2 = (mod[:, i] for i in range(6))

    o_b, o_a, o_g, o_sb, o_ga = 3072, 3080, 3088, 4112, 7184
    w_main = jnp.concatenate([w_in[:, :o_b], w_in[:, o_sb:o_ga], w_in[:, o_ga:], w_in[:, o_g:o_sb]],
                             axis=1).astype(BF16)
    w_ba = jnp.zeros((D_MODEL, LANES), F32).at[:, :2 * N_HEADS].set(w_in[:, o_b:o_g]).astype(BF16)
    avec = jnp.zeros((1, LANES), F32).at[0, N_HEADS:2 * N_HEADS].set(a_log)
    dtvec = jnp.zeros((1, LANES), F32).at[0, N_HEADS:2 * N_HEADS].set(dt_bias)

    p, p_ba = _inproj(x2, 1.0 + sc1, sh1, w_main, w_ba, seq)
    qkv = _conv(p, conv_w, bsz, seq)
    o_a = _delta(qkv, p, p_ba, avec, dtvec, dn_norm_w.reshape(1, HEAD_DIM), bsz, seq)
    o_b = _sb(p, bsz, seq)
    merged = _merge(o_a, o_b, p, w_pa.astype(BF16), w_pb.astype(BF16))
    x1, ht = _post1(merged, x2, w_o.astype(BF16), 1.0 + g1, ln1_g.reshape(1, D_MODEL),
                    ln1_b.reshape(1, D_MODEL), 1.0 + sc2, sh2, seq, alpha)

    keys = peer_keys.reshape(2 * N_HEADS, PEER_NKEYS, HEAD_DIM).astype(BF16)
    s1, e1, s2, e2, tau = _route(ht, peer_wq.T.astype(BF16), keys)
    yt = _dense(ht, peer_u.astype(BF16), peer_v.T.astype(BF16), s1, e1, s2, e2, tau)
    return _post2(yt, x1, 1.0 + g2, ln2_g.reshape(1, D_MODEL), ln2_b.reshape(1, D_MODEL), seq, alpha)


def kernel(x, c, w_ada, b_ada, w_in, conv_w, a_log, dt_bias, dn_norm_w, w_pa, w_pb, w_o, ln1_g, ln1_b,
           peer_wq, peer_keys, peer_u, peer_v, ln2_g, ln2_b):
    bsz, seq, _ = x.shape
    depth = w_ada.shape[0]
    alpha = (2 * depth) ** 0.25
    x2 = x.reshape(bsz * seq, D_MODEL)
    for l in range(depth):
        x2 = _layer(x2, c, bsz, seq, alpha, w_ada[l], b_ada[l], w_in[l], conv_w[l], a_log[l], dt_bias[l],
                    dn_norm_w[l], w_pa[l], w_pb[l], w_o[l], ln1_g[l], ln1_b[l], peer_wq[l], peer_keys[l],
                    peer_u[l], peer_v[l], ln2_g[l], ln2_b[l])
    return x2.reshape(bsz, seq, D_MODEL)
```

```python
import functools
import math

import jax
import jax.numpy as jnp
from jax import lax
from jax.experimental import pallas as pl
from jax.experimental.pallas import tpu as pltpu

F32 = jnp.float32
BF16 = jnp.bfloat16
HIGHEST = lax.Precision.HIGHEST

LN_EPS = 1e-5
NORM_EPS = 1e-6
D_MODEL = 2048
LANES = 128
CHUNK = 64
N_HEADS = 8
HEAD_DIM = 128
DN_CONV = 4
QKV_W = 3 * N_HEADS * HEAD_DIM
SB_QBLOCK = 128
PEER_NKEYS = 128
PEER_TOPK = 16
EXP_UNDERFLOW = -104.0
VMEM_LIMIT = 56 * 1024 * 1024

COL_DNQKV = 0
COL_SBQKV = 3072
COL_GATE_A = 6144
COL_GATE_B = 8192
COL_DNG = 10240
MAIN_W = 11264


def _cparams(*sem):
    return pltpu.CompilerParams(dimension_semantics=sem, vmem_limit_bytes=VMEM_LIMIT)


def _dot(a, b):
    return jnp.dot(a.astype(BF16), b.astype(BF16), preferred_element_type=F32)


def _dot_nt(a, b):
    return lax.dot_general(a.astype(BF16), b.astype(BF16), (((1,), (1,)), ((), ())),
                           preferred_element_type=F32)


def _silu(x):
    return x * jax.nn.sigmoid(x)


def _softplus(x):
    return jnp.maximum(x, 0.0) + jnp.log1p(jnp.exp(-jnp.abs(x)))


def _layernorm(x):
    mu = jnp.mean(x, axis=-1, keepdims=True)
    xc = x - mu
    var = jnp.mean(xc * xc, axis=-1, keepdims=True)
    return xc * lax.rsqrt(var + LN_EPS)


def _ada_kernel(c_ref, w_ref, b_ref, o_ref):
    s = _silu(c_ref[...])
    o_ref[...] = jnp.dot(s, w_ref[...], preferred_element_type=F32, precision=HIGHEST) + b_ref[...]


def _ada(c, w_ada, b_ada):
    bsz = c.shape[0]
    width = w_ada.shape[1]
    tn = 1024
    cp = jnp.zeros((8, D_MODEL), F32).at[:bsz].set(c)
    out = pl.pallas_call(
        _ada_kernel,
        grid=(width // tn,),
        in_specs=[pl.BlockSpec((8, D_MODEL), lambda j: (0, 0)),
                  pl.BlockSpec((D_MODEL, tn), lambda j: (0, j)),
                  pl.BlockSpec((1, tn), lambda j: (0, j))],
        out_specs=pl.BlockSpec((8, tn), lambda j: (0, j)),
        out_shape=jax.ShapeDtypeStruct((8, width), F32),
        compiler_params=_cparams("parallel"),
        name="ada",
    )(cp, w_ada, b_ada.reshape(1, width))
    return out[:bsz]


def _inproj_kernel(x_ref, scp_ref, sh_ref, w_ref, wba_ref, p_ref, pba_ref, h_scr):
    @pl.when(pl.program_id(1) == 0)
    def _():
        h = (_layernorm(x_ref[...]) * scp_ref[0] + sh_ref[0]).astype(BF16)
        h_scr[...] = h
        pba_ref[...] = jnp.dot(h, wba_ref[...], preferred_element_type=F32)

    p_ref[...] = jnp.dot(h_scr[...], w_ref[...], preferred_element_type=F32).astype(p_ref.dtype)


def _inproj(x2, scp, sh, w_main, w_ba, seq):
    n = x2.shape[0]
    tm = min(1024, seq)
    tn = 1024
    tpb = seq // tm
    return pl.pallas_call(
        _inproj_kernel,
        grid=(n // tm, MAIN_W // tn),
        in_specs=[pl.BlockSpec((tm, D_MODEL), lambda i, j: (i, 0)),
                  pl.BlockSpec((1, 1, D_MODEL), lambda i, j: (i // tpb, 0, 0)),
                  pl.BlockSpec((1, 1, D_MODEL), lambda i, j: (i // tpb, 0, 0)),
                  pl.BlockSpec((D_MODEL, tn), lambda i, j: (0, j)),
                  pl.BlockSpec((D_MODEL, LANES), lambda i, j: (0, 0))],
        out_specs=[pl.BlockSpec((tm, tn), lambda i, j: (i, j)),
                   pl.BlockSpec((tm, LANES), lambda i, j: (i, 0))],
        out_shape=[jax.ShapeDtypeStruct((n, MAIN_W), BF16),
                   jax.ShapeDtypeStruct((n, LANES), F32)],
        scratch_shapes=[pltpu.VMEM((tm, D_MODEL), BF16)],
        compiler_params=_cparams("parallel", "arbitrary"),
        name="inproj",
    )(x2, scp, sh, w_main, w_ba)


def _conv_kernel(x_ref, w_ref, o_ref, prev_scr):
    tt = x_ref.shape[0]

    @pl.when(pl.program_id(1) == 0)
    def _():
        prev_scr[...] = jnp.zeros_like(prev_scr)

    for g in range(QKV_W // LANES):
        cols = slice(g * LANES, (g + 1) * LANES)
        x = x_ref[:, cols].astype(F32)
        xp = jnp.concatenate([prev_scr[:, cols], x], axis=0)
        w = w_ref[:, cols]
        y = w[DN_CONV - 1:DN_CONV] * x
        for i in range(DN_CONV - 1):
            off = 8 - (DN_CONV - 1) + i
            y = y + w[i:i + 1] * xp[off:off + tt]
        prev_scr[:, cols] = x[tt - 8:]
        y = _silu(y)
        if g < 2 * N_HEADS:
            y = y * lax.rsqrt(jnp.sum(y * y, axis=-1, keepdims=True) + NORM_EPS)
        o_ref[:, cols] = y.astype(o_ref.dtype)


def _conv(p, conv_w, bsz, seq):
    tt = min(256, seq)
    tpb = seq // tt
    return pl.pallas_call(
        _conv_kernel,
        grid=(bsz, tpb),
        in_specs=[pl.BlockSpec((tt, QKV_W), lambda b, t: (b * tpb + t, COL_DNQKV // QKV_W)),
                  pl.BlockSpec((DN_CONV, QKV_W), lambda b, t: (0, 0))],
        out_specs=pl.BlockSpec((tt, QKV_W), lambda b, t: (b * tpb + t, 0)),
        out_shape=jax.ShapeDtypeStruct((bsz * seq, QKV_W), BF16),
        scratch_shapes=[pltpu.VMEM((8, QKV_W), F32)],
        compiler_params=_cparams("parallel", "arbitrary"),
        name="conv",
    )(p, conv_w)


def _bdot(a, b):
    return lax.dot_general(a.astype(BF16), b.astype(BF16), (((2,), (1,)), ((0,), (0,))),
                           preferred_element_type=F32)


def _bdot_nt(a, b):
    return lax.dot_general(a.astype(BF16), b.astype(BF16), (((2,), (2,)), ((0,), (0,))),
                           preferred_element_type=F32)


def _bdot_exact(a, b):
    return lax.dot_general(a, b, (((2,), (1,)), ((0,), (0,))), preferred_element_type=F32,
                           precision=HIGHEST)


def _delta_kernel(q_ref, k_ref, v_ref, gate_ref, ba_ref, avec_ref, dtvec_ref, nw_ref, o_ref,
                  s_scr, wq_scr, u_scr, akd_scr, cd_scr, beta_scr, gc_scr):
    tt = q_ref.shape[0]
    nc = tt // CHUNK
    h = pl.program_id(2)

    @pl.when(jnp.logical_and(pl.program_id(1) == 0, h == 0))
    def _():
        s_scr[...] = jnp.zeros_like(s_scr)

    r = lax.broadcasted_iota(jnp.int32, (nc, CHUNK, CHUNK), 1)
    c = lax.broadcasted_iota(jnp.int32, (nc, CHUNK, CHUNK), 2)
    incl = r >= c
    strict = r > c
    eye = (r == c).astype(F32)

    @pl.when(h == 0)
    def _():
        ba = ba_ref[...]
        beta_scr[...] = jax.nn.sigmoid(ba)
        g_all = -jnp.exp(avec_ref[...]) * _softplus(ba + dtvec_ref[...])
        gc_all = _bdot_exact(incl.astype(F32), g_all.reshape(nc, CHUNK, LANES))
        gc_scr[...] = gc_all.reshape(tt, LANES)

    lane = lax.broadcasted_iota(jnp.int32, (tt, LANES), 1)
    b_col = jnp.sum(jnp.where(lane == h, beta_scr[...], 0.0), axis=1, keepdims=True)
    gc_col = jnp.sum(jnp.where(lane == h + N_HEADS, gc_scr[...], 0.0), axis=1, keepdims=True)
    beta = jnp.broadcast_to(b_col, (tt, LANES)).reshape(nc, CHUNK, LANES)
    gc = jnp.broadcast_to(gc_col, (tt, LANES)).reshape(nc, CHUNK, LANES)
    gc_rows = jnp.stack([gc[ci].T[:CHUNK] for ci in range(nc)])

    q = q_ref[...].astype(F32).reshape(nc, CHUNK, HEAD_DIM) * (HEAD_DIM ** -0.5)
    k = k_ref[...].astype(F32).reshape(nc, CHUNK, HEAD_DIM)
    v = v_ref[...].astype(F32).reshape(nc, CHUNK, HEAD_DIM)
    decay = jnp.where(incl, jnp.exp(gc[:, :, :CHUNK] - gc_rows), 0.0)
    egc = jnp.exp(gc)
    gl = gc[:, CHUNK - 1:CHUNK, :]
    kb = k * beta
    p = jnp.where(strict, -(_bdot_nt(kb, k) * decay), 0.0)
    tm = eye + p
    for _ in range(CHUNK.bit_length() - 2):
        p = _bdot(p, p)
        tm = tm + _bdot(tm, p)
    wq_scr[h, :, :CHUNK, :] = _bdot(tm, kb * egc).astype(BF16)
    wq_scr[h, :, CHUNK:, :] = (q * egc).astype(BF16)
    u_scr[h] = _bdot(tm, v * beta)
    akd_scr[h, :, :CHUNK, :] = (_bdot_nt(q, k) * decay).astype(BF16)
    kd = k * jnp.exp(gl - gc)
    for ci in range(nc):
        akd_scr[h, ci, CHUNK:, :] = kd[ci].T.astype(BF16)
    cd_scr[h] = jnp.exp(gl)

    @pl.when(h == N_HEADS - 1)
    def _():
        def chunk(ci, carry):
            rows = pl.ds(pl.multiple_of(ci * CHUNK, CHUNK), CHUNK)
            heads = range(N_HEADS)
            ws = [jnp.dot(wq_scr[hh, ci], s_scr[hh].astype(BF16), preferred_element_type=F32)
                  for hh in heads]
            v_new = [(u_scr[hh, ci] - ws[hh][:CHUNK]).astype(BF16) for hh in heads]
            av = [jnp.dot(akd_scr[hh, ci], v_new[hh], preferred_element_type=F32) for hh in heads]
            for hh in heads:
                s_scr[hh] = s_scr[hh] * cd_scr[hh, ci] + av[hh][CHUNK:]
            for hh in heads:
                cols = slice(hh * HEAD_DIM, (hh + 1) * HEAD_DIM)
                o = ws[hh][CHUNK:] + av[hh][:CHUNK]
                on = o * lax.rsqrt(jnp.mean(o * o, axis=-1, keepdims=True) + NORM_EPS) * nw_ref[...]
                o_ref[rows, cols] = (on * _silu(gate_ref[rows, cols].astype(F32))).astype(o_ref.dtype)
            return carry

        lax.fori_loop(0, nc, chunk, 0)


def _delta(qkv, p, p_ba, avec, dtvec, nw, bsz, seq):
    tt = min(512, seq)
    nc = tt // CHUNK
    tpb = seq // tt
    hw = N_HEADS * HEAD_DIM
    row = lambda b, t, h: b * tpb + t
    return pl.pallas_call(
        _delta_kernel,
        grid=(bsz, tpb, N_HEADS),
        in_specs=[pl.BlockSpec((tt, HEAD_DIM), lambda b, t, h: (row(b, t, h), h)),
                  pl.BlockSpec((tt, HEAD_DIM), lambda b, t, h: (row(b, t, h), N_HEADS + h)),
                  pl.BlockSpec((tt, HEAD_DIM), lambda b, t, h: (row(b, t, h), 2 * N_HEADS + h)),
                  pl.BlockSpec((tt, hw), lambda b, t, h: (row(b, t, h), COL_DNG // hw)),
                  pl.BlockSpec((tt, LANES), lambda b, t, h: (row(b, t, h), 0)),
                  pl.BlockSpec((1, LANES), lambda b, t, h: (0, 0)),
                  pl.BlockSpec((1, LANES), lambda b, t, h: (0, 0)),
                  pl.BlockSpec((1, HEAD_DIM), lambda b, t, h: (0, 0))],
        out_specs=pl.BlockSpec((tt, hw), lambda b, t, h: (row(b, t, h), 0)),
        out_shape=jax.ShapeDtypeStruct((bsz * seq, hw), BF16),
        scratch_shapes=[pltpu.VMEM((N_HEADS, HEAD_DIM, HEAD_DIM), F32),
                        pltpu.VMEM((N_HEADS, nc, 2 * CHUNK, HEAD_DIM), BF16),
                        pltpu.VMEM((N_HEADS, nc, CHUNK, HEAD_DIM), F32),
                        pltpu.VMEM((N_HEADS, nc, CHUNK + HEAD_DIM, CHUNK), BF16),
                        pltpu.VMEM((N_HEADS, nc, 1, HEAD_DIM), F32),
                        pltpu.VMEM((tt, LANES), F32),
                        pltpu.VMEM((tt, LANES), F32)],
        compiler_params=_cparams("parallel", "arbitrary", "arbitrary"),
        name="delta",
    )(qkv, qkv, qkv, p, p_ba, avec, dtvec, nw)


def _sb_kernel(q_ref, k_ref, v_ref, o_ref, acc_scr, run_scr):
    blk = SB_QBLOCK
    n_chain = q_ref.shape[0] // blk
    chains = range(n_chain)
    first = pl.program_id(2) * n_chain
    scale = HEAD_DIM ** -0.5
    row = lax.broadcasted_iota(jnp.int32, (blk, blk), 0)
    col = lax.broadcasted_iota(jnp.int32, (blk, blk), 1)
    below = col < row
    r2 = lax.broadcasted_iota(jnp.int32, (blk, 2 * blk), 0)
    c2 = lax.broadcasted_iota(jnp.int32, (blk, 2 * blk), 1)
    cum_m = jnp.where((r2 > c2) | (c2 >= blk), 1.0, 0.0).astype(BF16)
    acc_scr[...] = jnp.zeros_like(acc_scr)
    run_scr[...] = jnp.zeros_like(run_scr)

    def body(carry):
        dist, _ = carry
        kbs = [first + c - dist for c in chains]
        ks = [pl.ds(pl.multiple_of(jnp.maximum(kb, 0) * blk, blk), blk) for kb in kbs]
        z = [_dot_nt(q_ref[c * blk:(c + 1) * blk, :], k_ref[ks[c], :]) * scale for c in chains]
        sp = [_softplus(z[c]) for c in chains]
        valid = [jnp.logical_and(kbs[c] >= 0, jnp.logical_or(below, dist > 0)) for c in chains]
        ls = [jnp.where(valid[c], -sp[c], 0.0) for c in chains]
        hi = [ls[c].astype(BF16) for c in chains]
        lo = [(ls[c] - hi[c].astype(F32)).astype(BF16) for c in chains]
        ct = [jnp.dot(hi[c], cum_m, preferred_element_type=F32)
              + jnp.dot(lo[c], cum_m, preferred_element_type=F32) for c in chains]
        live = jnp.full((blk, blk), -jnp.inf, F32)
        for c in chains:
            run = run_scr[c]
            a = jnp.where(valid[c], jnp.exp((z[c] - sp[c]) + ct[c][:, :blk] + run), 0.0)
            acc_scr[c] += _dot(a, v_ref[ks[c], :])
            run = run + ct[c][:, blk:]
            run_scr[c] = run
            live = jnp.maximum(live, jnp.where(kbs[c] >= 1, run, -jnp.inf))
        return dist + 1, jnp.max(live) >= EXP_UNDERFLOW

    lax.while_loop(lambda carry: carry[1], body, (jnp.int32(0), True))
    for c in chains:
        o_ref[c * blk:(c + 1) * blk, :] = acc_scr[c].astype(o_ref.dtype)


def _sb(p, bsz, seq):
    n_chain = min(8, seq // SB_QBLOCK)
    tq = n_chain * SB_QBLOCK
    nq = seq // tq
    cq = COL_SBQKV // HEAD_DIM
    return pl.pallas_call(
        _sb_kernel,
        grid=(bsz, N_HEADS, nq),
        in_specs=[pl.BlockSpec((tq, HEAD_DIM), lambda b, h, i: (b * nq + i, cq + h)),
                  pl.BlockSpec((seq, HEAD_DIM), lambda b, h, i: (b, cq + N_HEADS + h)),
                  pl.BlockSpec((seq, HEAD_DIM), lambda b, h, i: (b, cq + 2 * N_HEADS + h))],
        out_specs=pl.BlockSpec((tq, HEAD_DIM), lambda b, h, i: (b * nq + i, h)),
        out_shape=jax.ShapeDtypeStruct((bsz * seq, N_HEADS * HEAD_DIM), BF16),
        scratch_shapes=[pltpu.VMEM((n_chain, SB_QBLOCK, HEAD_DIM), F32),
                        pltpu.VMEM((n_chain, SB_QBLOCK, SB_QBLOCK), F32)],
        compiler_params=_cparams("parallel", "parallel", "arbitrary"),
        name="sb",
    )(p, p, p)


def _merge_kernel(oa_ref, ob_ref, ga_ref, gb_ref, wpa_ref, wpb_ref, o_ref):
    ya = jnp.dot(oa_ref[...], wpa_ref[...], preferred_element_type=F32)
    yb = jnp.dot(ob_ref[...], wpb_ref[...], preferred_element_type=F32)
    m = (jax.nn.sigmoid(ga_ref[...].astype(F32)) * ya
         + jax.nn.sigmoid(gb_ref[...].astype(F32)) * yb)
    o_ref[...] = m.astype(o_ref.dtype)


def _merge(o_a, o_b, p, w_pa, w_pb):
    n = o_a.shape[0]
    tm = 512
    hw = N_HEADS * HEAD_DIM
    return pl.pallas_call(
        _merge_kernel,
        grid=(n // tm,),
        in_specs=[pl.BlockSpec((tm, hw), lambda i: (i, 0)),
                  pl.BlockSpec((tm, hw), lambda i: (i, 0)),
                  pl.BlockSpec((tm, D_MODEL), lambda i: (i, COL_GATE_A // D_MODEL)),
                  pl.BlockSpec((tm, D_MODEL), lambda i: (i, COL_GATE_B // D_MODEL)),
                  pl.BlockSpec((hw, D_MODEL), lambda i: (0, 0)),
                  pl.BlockSpec((hw, D_MODEL), lambda i: (0, 0))],
        out_specs=pl.BlockSpec((tm, D_MODEL), lambda i: (i, 0)),
        out_shape=jax.ShapeDtypeStruct((n, D_MODEL), BF16),
        compiler_params=_cparams("parallel"),
        name="merge",
    )(o_a, o_b, p, p, w_pa, w_pb)


def _post1_kernel(alpha, m_ref, x_ref, wo_ref, g1p_ref, lg_ref, lb_ref, scp_ref, sh_ref,
                  x1_ref, ht_ref):
    y = jnp.dot(m_ref[...], wo_ref[...], preferred_element_type=F32)
    x1 = _layernorm(alpha * x_ref[...] + g1p_ref[0] * y) * lg_ref[...] + lb_ref[...]
    x1_ref[...] = x1
    h2 = _layernorm(x1) * scp_ref[0] + sh_ref[0]
    ht_ref[...] = h2.T.astype(ht_ref.dtype)


def _post1(merged, x2, w_o, g1p, ln_g, ln_b, scp2, sh2, seq, alpha):
    n = x2.shape[0]
    tm = min(512, seq)
    tpb = seq // tm
    bvec = lambda i: (i // tpb, 0, 0)
    return pl.pallas_call(
        functools.partial(_post1_kernel, alpha),
        grid=(n // tm,),
        in_specs=[pl.BlockSpec((tm, D_MODEL), lambda i: (i, 0)),
                  pl.BlockSpec((tm, D_MODEL), lambda i: (i, 0)),
                  pl.BlockSpec((D_MODEL, D_MODEL), lambda i: (0, 0)),
                  pl.BlockSpec((1, 1, D_MODEL), bvec),
                  pl.BlockSpec((1, D_MODEL), lambda i: (0, 0)),
                  pl.BlockSpec((1, D_MODEL), lambda i: (0, 0)),
                  pl.BlockSpec((1, 1, D_MODEL), bvec),
                  pl.BlockSpec((1, 1, D_MODEL), bvec)],
        out_specs=[pl.BlockSpec((tm, D_MODEL), lambda i: (i, 0)),
                   pl.BlockSpec((D_MODEL, tm), lambda i: (0, i))],
        out_shape=[jax.ShapeDtypeStruct((n, D_MODEL), F32),
                   jax.ShapeDtypeStruct((D_MODEL, n), BF16)],
        compiler_params=_cparams("parallel"),
        name="post1",
    )(merged, x2, w_o, g1p, ln_g, ln_b, scp2, sh2)


def _extract_top(lists, count, out_scrs):
    lists = list(lists)
    rids = [lax.broadcasted_iota(jnp.int32, s.shape, 0).astype(F32) for s in lists]
    last = [None] * len(lists)
    for r in range(count):
        for i, s in enumerate(lists):
            m = jnp.max(s, axis=0, keepdims=True)
            first = jnp.min(jnp.where(s == m, rids[i], float(s.shape[0])), axis=0, keepdims=True)
            lists[i] = jnp.where(rids[i] == first, -jnp.inf, s)
            if out_scrs[i] is not None:
                out_scrs[i][r:r + 1, :] = m
            last[i] = m
    return last


def _route_kernel(ht_ref, wqt_ref, keys_ref, s1_ref, e1_ref, s2_ref, e2_ref, tau_ref,
                  st_scr, v1_scr, v2_scr, cand_scr):
    qt = jnp.dot(wqt_ref[...], ht_ref[...], preferred_element_type=F32)
    for hp in range(2 * N_HEADS):
        st_scr[hp] = jnp.dot(keys_ref[hp], qt[hp * HEAD_DIM:(hp + 1) * HEAD_DIM].astype(BF16),
                             preferred_element_type=F32)
    rid8 = lax.broadcasted_iota(jnp.int32, (8, ht_ref.shape[1]), 0)
    neg = -jnp.inf

    def head(h, carry):
        s1 = st_scr[2 * h]
        s2 = st_scr[2 * h + 1]
        _extract_top([s1, s2], PEER_TOPK, [v1_scr, v2_scr])
        cand_scr[0:16, :] = v1_scr[0:1, :] + v2_scr[0:16, :]
        lo8 = v2_scr[0:8, :]
        for a in range(1, 8):
            nb = PEER_TOPK // (a + 1)
            cand_scr[8 + 8 * a:16 + 8 * a, :] = jnp.where(rid8 < nb, v1_scr[a:a + 1, :] + lo8, neg)
        cand_scr[72:80, :] = v1_scr[8:16, :] + v2_scr[0:1, :]
        cand = cand_scr[...]
        tau, = _extract_top([cand], PEER_TOPK, [None])
        top = v1_scr[0:1, :] + v2_scr[0:1, :]
        z = jnp.sum(jnp.where(cand >= tau, jnp.exp(cand - top), 0.0), axis=0, keepdims=True)
        s1_ref[h] = s1
        s2_ref[h] = s2
        e1_ref[h] = jnp.exp(s1 - v1_scr[0:1, :])
        e2_ref[h] = jnp.exp(s2 - v2_scr[0:1, :]) * (1.0 / z)
        tau_ref[pl.ds(h, 1), :] = tau
        return carry

    lax.fori_loop(0, N_HEADS, head, 0)


def _route(ht, wqt, keys):
    n = ht.shape[1]
    tn = 2 * LANES
    tab = jax.ShapeDtypeStruct((N_HEADS, PEER_NKEYS, n), F32)
    tab_spec = pl.BlockSpec((N_HEADS, PEER_NKEYS, tn), lambda t: (0, 0, t))
    return pl.pallas_call(
        _route_kernel,
        grid=(n // tn,),
        in_specs=[pl.BlockSpec((D_MODEL, tn), lambda t: (0, t)),
                  pl.BlockSpec((D_MODEL, D_MODEL), lambda t: (0, 0)),
                  pl.BlockSpec((2 * N_HEADS, PEER_NKEYS, HEAD_DIM), lambda t: (0, 0, 0))],
        out_specs=[tab_spec, tab_spec, tab_spec, tab_spec,
                   pl.BlockSpec((N_HEADS, tn), lambda t: (0, t))],
        out_shape=[tab, tab, tab, tab, jax.ShapeDtypeStruct((N_HEADS, n), F32)],
        scratch_shapes=[pltpu.VMEM((2 * N_HEADS, PEER_NKEYS, tn), F32),
                        pltpu.VMEM((PEER_TOPK, tn), F32),
                        pltpu.VMEM((PEER_TOPK, tn), F32),
                        pltpu.VMEM((80, tn), F32)],
        compiler_params=_cparams("parallel"),
        name="route",
    )(ht, wqt, keys)


EXPERTS_PER_STEP = 1024


def _gelu(x):
    return 0.5 * x * (1.0 + lax.erf(x * (2.0 ** -0.5)))


def _dense_kernel(n_chunks, ht_ref, u_ref, vt_ref, s1_ref, e1_ref, s2_ref, e2_ref, tau_ref, yt_ref,
                  act0_scr, act1_scr, wa0_scr, wa1_scr):
    tn = ht_ref.shape[1]
    g = pl.program_id(0)

    @pl.when(g == 0)
    def _():
        for ref in (act0_scr, act1_scr, wa0_scr, wa1_scr):
            ref[...] = jnp.zeros_like(ref)

    @pl.when(jnp.logical_or(g == 0, (g + 2 * n_chunks - 2) % n_chunks == 0))
    def _():
        yt_ref[...] = jnp.zeros_like(yt_ref)

    n_first = EXPERTS_PER_STEP // PEER_NKEYS
    j_rows = 16
    j_sub = j_rows // 8
    pieces = 4
    j_per_piece = PEER_NKEYS // j_rows // pieces
    a_rows = EXPERTS_PER_STEP // pieces
    c_rows = D_MODEL // pieces

    def piece(act_a, act_b, wa_b, wa_c, jq, carry):
        rows = pl.ds(pl.multiple_of(jq * a_rows, a_rows), a_rows)
        act_a[rows, :] = _gelu(jnp.dot(u_ref[rows, :], ht_ref[...], preferred_element_type=F32))

        for js in range(j_per_piece):
            j0 = pl.multiple_of((jq * j_per_piece + js) * j_rows, j_rows)
            for lt in range(tn // LANES):
                cols = slice(lt * LANES, (lt + 1) * LANES)
                w = [jnp.zeros((j_sub, 8, LANES), F32) for _ in range(n_first)]
                for h in range(N_HEADS):
                    s2q = s2_ref[h, pl.ds(j0, j_rows), cols].reshape(j_sub, 8, LANES)
                    e2q = e2_ref[h, pl.ds(j0, j_rows), cols].reshape(j_sub, 8, LANES)
                    tau = jnp.broadcast_to(tau_ref[h:h + 1, cols], (8, LANES))
                    for ii in range(n_first):
                        s1b = jnp.broadcast_to(s1_ref[h, ii:ii + 1, cols], (8, LANES))
                        e1b = jnp.broadcast_to(e1_ref[h, ii:ii + 1, cols], (8, LANES))
                        w[ii] = w[ii] + jnp.where(s1b + s2q >= tau, e1b * e2q, 0.0)
                for ii in range(n_first):
                    er = pl.ds(ii * PEER_NKEYS + j0, j_rows)
                    wa_b[er, cols] = (w[ii].reshape(j_rows, LANES) * act_b[er, cols]).astype(BF16)

        rows = pl.ds(pl.multiple_of(jq * c_rows, c_rows), c_rows)
        yt_ref[rows, :] += jnp.dot(vt_ref[rows, :], wa_c[...], preferred_element_type=F32)
        return carry

    @pl.when(g % 2 == 0)
    def _():
        lax.fori_loop(0, pieces, functools.partial(piece, act0_scr, act1_scr, wa1_scr, wa0_scr), 0)

    @pl.when(g % 2 == 1)
    def _():
        lax.fori_loop(0, pieces, functools.partial(piece, act1_scr, act0_scr, wa0_scr, wa1_scr), 0)


def _dense(ht, u_tab, vt_tab, s1, e1, s2, e2, tau):
    n = ht.shape[1]
    tn = 512
    n_chunks = u_tab.shape[0] // EXPERTS_PER_STEP
    total = (n // tn) * n_chunks
    rows_per_step = EXPERTS_PER_STEP // PEER_NKEYS
    job_a = lambda g: jnp.minimum(g, total - 1)
    job_b = lambda g: jnp.clip(g - 1, 0, total - 1)
    job_c = lambda g: jnp.clip(g - 2, 0, total - 1)
    row_spec = pl.BlockSpec((N_HEADS, rows_per_step, tn),
                            lambda g: (0, job_b(g) % n_chunks, job_b(g) // n_chunks))
    tab_spec = pl.BlockSpec((N_HEADS, PEER_NKEYS, tn), lambda g: (0, 0, job_b(g) // n_chunks))
    return pl.pallas_call(
        functools.partial(_dense_kernel, n_chunks),
        grid=(total + 2,),
        in_specs=[pl.BlockSpec((D_MODEL, tn), lambda g: (0, job_a(g) // n_chunks)),
                  pl.BlockSpec((EXPERTS_PER_STEP, D_MODEL), lambda g: (job_a(g) % n_chunks, 0)),
                  pl.BlockSpec((D_MODEL, EXPERTS_PER_STEP), lambda g: (0, job_c(g) % n_chunks)),
                  row_spec, row_spec, tab_spec, tab_spec,
                  pl.BlockSpec((N_HEADS, tn), lambda g: (0, job_b(g) // n_chunks))],
        out_specs=pl.BlockSpec((D_MODEL, tn), lambda g: (0, job_c(g) // n_chunks)),
        out_shape=jax.ShapeDtypeStruct((D_MODEL, n), F32),
        scratch_shapes=[pltpu.VMEM((EXPERTS_PER_STEP, tn), F32),
                        pltpu.VMEM((EXPERTS_PER_STEP, tn), F32),
                        pltpu.VMEM((EXPERTS_PER_STEP, tn), BF16),
                        pltpu.VMEM((EXPERTS_PER_STEP, tn), BF16)],
        compiler_params=_cparams("arbitrary"),
        name="dense",
    )(ht, u_tab, vt_tab, s1, e1, s2, e2, tau)


def _post2_kernel(alpha, yt_ref, x_ref, g2p_ref, lg_ref, lb_ref, o_ref):
    y = yt_ref[...].T
    o_ref[...] = _layernorm(alpha * x_ref[...] + g2p_ref[0] * y) * lg_ref[...] + lb_ref[...]


def _post2(yt, x1, g2p, ln_g, ln_b, seq, alpha):
    n = x1.shape[0]
    tm = min(512, seq)
    tpb = seq // tm
    return pl.pallas_call(
        functools.partial(_post2_kernel, alpha),
        grid=(n // tm,),
        in_specs=[pl.BlockSpec((D_MODEL, tm), lambda i: (0, i)),
                  pl.BlockSpec((tm, D_MODEL), lambda i: (i, 0)),
                  pl.BlockSpec((1, 1, D_MODEL), lambda i: (i // tpb, 0, 0)),
                  pl.BlockSpec((1, D_MODEL), lambda i: (0, 0)),
                  pl.BlockSpec((1, D_MODEL), lambda i: (0, 0))],
        out_specs=pl.BlockSpec((tm, D_MODEL), lambda i: (i, 0)),
        out_shape=jax.ShapeDtypeStruct((n, D_MODEL), F32),
        compiler_params=_cparams("parallel"),
        name="post2",
    )(yt, x1, g2p, ln_g, ln_b)


def _layer(x2, c, bsz, seq, alpha, w_ada, b_ada, w_in, conv_w, a_log, dt_bias, dn_norm_w, w_pa,
           w_pb, w_o, ln1_g, ln1_b, peer_wq, peer_keys, peer_u, peer_v, ln2_g, ln2_b):
    mod = _ada(c, w_ada, b_ada).reshape(bsz, 6, 1, D_MODEL)
    sh1, sc1, g1, sh2, sc2, g2 = (mod[:, i] for i in range(6))

    o_b, o_a, o_g, o_sb, o_ga = 3072, 3080, 3088, 4112, 7184
    w_main = jnp.concatenate([w_in[:, :o_b], w_in[:, o_sb:o_ga], w_in[:, o_ga:], w_in[:, o_g:o_sb]],
                             axis=1).astype(BF16)
    w_ba = jnp.zeros((D_MODEL, LANES), F32).at[:, :2 * N_HEADS].set(w_in[:, o_b:o_g]).astype(BF16)
    avec = jnp.zeros((1, LANES), F32).at[0, N_HEADS:2 * N_HEADS].set(a_log)
    dtvec = jnp.zeros((1, LANES), F32).at[0, N_HEADS:2 * N_HEADS].set(dt_bias)

    p, p_ba = _inproj(x2, 1.0 + sc1, sh1, w_main, w_ba, seq)
    qkv = _conv(p, conv_w, bsz, seq)
    o_a = _delta(qkv, p, p_ba, avec, dtvec, dn_norm_w.reshape(1, HEAD_DIM), bsz, seq)
    o_b = _sb(p, bsz, seq)
    merged = _merge(o_a, o_b, p, w_pa.astype(BF16), w_pb.astype(BF16))
    x1, ht = _post1(merged, x2, w_o.astype(BF16), 1.0 + g1, ln1_g.reshape(1, D_MODEL),
                    ln1_b.reshape(1, D_MODEL), 1.0 + sc2, sh2, seq, alpha)

    keys = peer_keys.reshape(2 * N_HEADS, PEER_NKEYS, HEAD_DIM).astype(BF16)
    s1, e1, s2, e2, tau = _route(ht, peer_wq.T.astype(BF16), keys)
    yt = _dense(ht, peer_u.astype(BF16), peer_v.T.astype(BF16), s1, e1, s2, e2, tau)
    return _post2(yt, x1, 1.0 + g2, ln2_g.reshape(1, D_MODEL), ln2_b.reshape(1, D_MODEL), seq, alpha)


def kernel(x, c, w_ada, b_ada, w_in, conv_w, a_log, dt_bias, dn_norm_w, w_pa, w_pb, w_o, ln1_g, ln1_b,
           peer_wq, peer_keys, peer_u, peer_v, ln2_g, ln2_b):
    bsz, seq, _ = x.shape
    depth = w_ada.shape[0]
    alpha = (2 * depth) ** 0.25
    x2 = x.reshape(bsz * seq, D_MODEL)
    for l in range(depth):
        x2 = _layer(x2, c, bsz, seq, alpha, w_ada[l], b_ada[l], w_in[l], conv_w[l], a_log[l], dt_bias[l],
                    dn_norm_w[l], w_pa[l], w_pb[l], w_o[l], ln1_g[l], ln1_b[l], peer_wq[l], peer_keys[l],
                    peer_u[l], peer_v[l], ln2_g[l], ln2_b[l])
    return x2.reshape(bsz, seq, D_MODEL)
```

```python
import functools
import math

import jax
import jax.numpy as jnp
from jax import lax
from jax.experimental import pallas as pl
from jax.experimental.pallas import tpu as pltpu

F32 = jnp.float32
BF16 = jnp.bfloat16
HIGHEST = lax.Precision.HIGHEST

LN_EPS = 1e-5
NORM_EPS = 1e-6
D_MODEL = 2048
LANES = 128
CHUNK = 64
N_HEADS = 8
HEAD_DIM = 128
DN_CONV = 4
QKV_W = 3 * N_HEADS * HEAD_DIM
SB_QBLOCK = 128
PEER_NKEYS = 128
PEER_TOPK = 16
EXP_UNDERFLOW = -104.0
VMEM_LIMIT = 56 * 1024 * 1024

COL_DNQKV = 0
COL_SBQKV = 3072
COL_GATE_A = 6144
COL_GATE_B = 8192
COL_DNG = 10240
MAIN_W = 11264


def _cparams(*sem):
    return pltpu.CompilerParams(dimension_semantics=sem, vmem_limit_bytes=VMEM_LIMIT)


def _dot(a, b):
    return jnp.dot(a.astype(BF16), b.astype(BF16), preferred_element_type=F32)


def _dot_nt(a, b):
    return lax.dot_general(a.astype(BF16), b.astype(BF16), (((1,), (1,)), ((), ())),
                           preferred_element_type=F32)


def _silu(x):
    return x * jax.nn.sigmoid(x)


def _softplus(x):
    return jnp.maximum(x, 0.0) + jnp.log1p(jnp.exp(-jnp.abs(x)))


def _layernorm(x):
    mu = jnp.mean(x, axis=-1, keepdims=True)
    xc = x - mu
    var = jnp.mean(xc * xc, axis=-1, keepdims=True)
    return xc * lax.rsqrt(var + LN_EPS)


def _ada_kernel(c_ref, w_ref, b_ref, o_ref):
    s = _silu(c_ref[...])
    o_ref[...] = jnp.dot(s, w_ref[...], preferred_element_type=F32, precision=HIGHEST) + b_ref[...]


def _ada(c, w_ada, b_ada):
    bsz = c.shape[0]
    width = w_ada.shape[1]
    tn = 1024
    cp = jnp.zeros((8, D_MODEL), F32).at[:bsz].set(c)
    out = pl.pallas_call(
        _ada_kernel,
        grid=(width // tn,),
        in_specs=[pl.BlockSpec((8, D_MODEL), lambda j: (0, 0)),
                  pl.BlockSpec((D_MODEL, tn), lambda j: (0, j)),
                  pl.BlockSpec((1, tn), lambda j: (0, j))],
        out_specs=pl.BlockSpec((8, tn), lambda j: (0, j)),
        out_shape=jax.ShapeDtypeStruct((8, width), F32),
        compiler_params=_cparams("parallel"),
        name="ada",
    )(cp, w_ada, b_ada.reshape(1, width))
    return out[:bsz]


def _inproj_kernel(x_ref, scp_ref, sh_ref, w_ref, wba_ref, p_ref, pba_ref, h_scr):
    @pl.when(pl.program_id(1) == 0)
    def _():
        h = (_layernorm(x_ref[...]) * scp_ref[0] + sh_ref[0]).astype(BF16)
        h_scr[...] = h
        pba_ref[...] = jnp.dot(h, wba_ref[...], preferred_element_type=F32)

    p_ref[...] = jnp.dot(h_scr[...], w_ref[...], preferred_element_type=F32).astype(p_ref.dtype)


def _inproj(x2, scp, sh, w_main, w_ba, seq):
    n = x2.shape[0]
    tm = min(1024, seq)
    tn = 1024
    tpb = seq // tm
    return pl.pallas_call(
        _inproj_kernel,
        grid=(n // tm, MAIN_W // tn),
        in_specs=[pl.BlockSpec((tm, D_MODEL), lambda i, j: (i, 0)),
                  pl.BlockSpec((1, 1, D_MODEL), lambda i, j: (i // tpb, 0, 0)),
                  pl.BlockSpec((1, 1, D_MODEL), lambda i, j: (i // tpb, 0, 0)),
                  pl.BlockSpec((D_MODEL, tn), lambda i, j: (0, j)),
                  pl.BlockSpec((D_MODEL, LANES), lambda i, j: (0, 0))],
        out_specs=[pl.BlockSpec((tm, tn), lambda i, j: (i, j)),
                   pl.BlockSpec((tm, LANES), lambda i, j: (i, 0))],
        out_shape=[jax.ShapeDtypeStruct((n, MAIN_W), BF16),
                   jax.ShapeDtypeStruct((n, LANES), F32)],
        scratch_shapes=[pltpu.VMEM((tm, D_MODEL), BF16)],
        compiler_params=_cparams("parallel", "arbitrary"),
        name="inproj",
    )(x2, scp, sh, w_main, w_ba)


def _conv_kernel(x_ref, w_ref, o_ref, prev_scr):
    tt = x_ref.shape[0]

    @pl.when(pl.program_id(1) == 0)
    def _():
        prev_scr[...] = jnp.zeros_like(prev_scr)

    for g in range(QKV_W // LANES):
        cols = slice(g * LANES, (g + 1) * LANES)
        x = x_ref[:, cols].astype(F32)
        xp = jnp.concatenate([prev_scr[:, cols], x], axis=0)
        w = w_ref[:, cols]
        y = w[DN_CONV - 1:DN_CONV] * x
        for i in range(DN_CONV - 1):
            off = 8 - (DN_CONV - 1) + i
            y = y + w[i:i + 1] * xp[off:off + tt]
        prev_scr[:, cols] = x[tt - 8:]
        y = _silu(y)
        if g < 2 * N_HEADS:
            y = y * lax.rsqrt(jnp.sum(y * y, axis=-1, keepdims=True) + NORM_EPS)
        o_ref[:, cols] = y.astype(o_ref.dtype)


def _conv(p, conv_w, bsz, seq):
    tt = min(256, seq)
    tpb = seq // tt
    return pl.pallas_call(
        _conv_kernel,
        grid=(bsz, tpb),
        in_specs=[pl.BlockSpec((tt, QKV_W), lambda b, t: (b * tpb + t, COL_DNQKV // QKV_W)),
                  pl.BlockSpec((DN_CONV, QKV_W), lambda b, t: (0, 0))],
        out_specs=pl.BlockSpec((tt, QKV_W), lambda b, t: (b * tpb + t, 0)),
        out_shape=jax.ShapeDtypeStruct((bsz * seq, QKV_W), BF16),
        scratch_shapes=[pltpu.VMEM((8, QKV_W), F32)],
        compiler_params=_cparams("parallel", "arbitrary"),
        name="conv",
    )(p, conv_w)


def _bdot(a, b):
    return lax.dot_general(a.astype(BF16), b.astype(BF16), (((2,), (1,)), ((0,), (0,))),
                           preferred_element_type=F32)


def _bdot_nt(a, b):
    return lax.dot_general(a.astype(BF16), b.astype(BF16), (((2,), (2,)), ((0,), (0,))),
                           preferred_element_type=F32)


def _bdot_exact(a, b):
    return lax.dot_general(a, b, (((2,), (1,)), ((0,), (0,))), preferred_element_type=F32,
                           precision=HIGHEST)


def _delta_kernel(q_ref, k_ref, v_ref, gate_ref, ba_ref, avec_ref, dtvec_ref, nw_ref, o_ref,
                  s_scr, wq_scr, u_scr, akd_scr, cd_scr, beta_scr, gc_scr):
    tt = q_ref.shape[0]
    nc = tt // CHUNK
    h = pl.program_id(2)

    @pl.when(jnp.logical_and(pl.program_id(1) == 0, h == 0))
    def _():
        s_scr[...] = jnp.zeros_like(s_scr)

    r = lax.broadcasted_iota(jnp.int32, (nc, CHUNK, CHUNK), 1)
    c = lax.broadcasted_iota(jnp.int32, (nc, CHUNK, CHUNK), 2)
    incl = r >= c
    strict = r > c
    eye = (r == c).astype(F32)

    @pl.when(h == 0)
    def _():
        ba = ba_ref[...]
        beta_scr[...] = jax.nn.sigmoid(ba)
        g_all = -jnp.exp(avec_ref[...]) * _softplus(ba + dtvec_ref[...])
        gc_all = _bdot_exact(incl.astype(F32), g_all.reshape(nc, CHUNK, LANES))
        gc_scr[...] = gc_all.reshape(tt, LANES)

    lane = lax.broadcasted_iota(jnp.int32, (tt, LANES), 1)
    b_col = jnp.sum(jnp.where(lane == h, beta_scr[...], 0.0), axis=1, keepdims=True)
    gc_col = jnp.sum(jnp.where(lane == h + N_HEADS, gc_scr[...], 0.0), axis=1, keepdims=True)
    beta = jnp.broadcast_to(b_col, (tt, LANES)).reshape(nc, CHUNK, LANES)
    gc = jnp.broadcast_to(gc_col, (tt, LANES)).reshape(nc, CHUNK, LANES)
    gc_rows = jnp.stack([gc[ci].T[:CHUNK] for ci in range(nc)])

    q = q_ref[...].astype(F32).reshape(nc, CHUNK, HEAD_DIM) * (HEAD_DIM ** -0.5)
    k = k_ref[...].astype(F32).reshape(nc, CHUNK, HEAD_DIM)
    v = v_ref[...].astype(F32).reshape(nc, CHUNK, HEAD_DIM)
    decay = jnp.where(incl, jnp.exp(gc[:, :, :CHUNK] - gc_rows), 0.0)
    egc = jnp.exp(gc)
    gl = gc[:, CHUNK - 1:CHUNK, :]
    kb = k * beta
    p = jnp.where(strict, -(_bdot_nt(kb, k) * decay), 0.0)
    tm = eye + p
    for _ in range(CHUNK.bit_length() - 2):
        p = _bdot(p, p)
        tm = tm + _bdot(tm, p)
    wq_scr[h, :, :CHUNK, :] = _bdot(tm, kb * egc).astype(BF16)
    wq_scr[h, :, CHUNK:, :] = (q * egc).astype(BF16)
    u_scr[h] = _bdot(tm, v * beta)
    akd_scr[h, :, :CHUNK, :] = (_bdot_nt(q, k) * decay).astype(BF16)
    kd = k * jnp.exp(gl - gc)
    for ci in range(nc):
        akd_scr[h, ci, CHUNK:, :] = kd[ci].T.astype(BF16)
    cd_scr[h] = jnp.exp(gl)

    @pl.when(h == N_HEADS - 1)
    def _():
        def chunk(ci, carry):
            rows = pl.ds(pl.multiple_of(ci * CHUNK, CHUNK), CHUNK)
            heads = range(N_HEADS)
            ws = [jnp.dot(wq_scr[hh, ci], s_scr[hh].astype(BF16), preferred_element_type=F32)
                  for hh in heads]
            v_new = [(u_scr[hh, ci] - ws[hh][:CHUNK]).astype(BF16) for hh in heads]
            av = [jnp.dot(akd_scr[hh, ci], v_new[hh], preferred_element_type=F32) for hh in heads]
            for hh in heads:
                s_scr[hh] = s_scr[hh] * cd_scr[hh, ci] + av[hh][CHUNK:]
            for hh in heads:
                cols = slice(hh * HEAD_DIM, (hh + 1) * HEAD_DIM)
                o = ws[hh][CHUNK:] + av[hh][:CHUNK]
                on = o * lax.rsqrt(jnp.mean(o * o, axis=-1, keepdims=True) + NORM_EPS) * nw_ref[...]
                o_ref[rows, cols] = (on * _silu(gate_ref[rows, cols].astype(F32))).astype(o_ref.dtype)
            return carry

        lax.fori_loop(0, nc, chunk, 0)


def _delta(qkv, p, p_ba, avec, dtvec, nw, bsz, seq):
    tt = min(1024, seq)
    nc = tt // CHUNK
    tpb = seq // tt
    hw = N_HEADS * HEAD_DIM
    row = lambda b, t, h: b * tpb + t
    return pl.pallas_call(
        _delta_kernel,
        grid=(bsz, tpb, N_HEADS),
        in_specs=[pl.BlockSpec((tt, HEAD_DIM), lambda b, t, h: (row(b, t, h), h)),
                  pl.BlockSpec((tt, HEAD_DIM), lambda b, t, h: (row(b, t, h), N_HEADS + h)),
                  pl.BlockSpec((tt, HEAD_DIM), lambda b, t, h: (row(b, t, h), 2 * N_HEADS + h)),
                  pl.BlockSpec((tt, hw), lambda b, t, h: (row(b, t, h), COL_DNG // hw)),
                  pl.BlockSpec((tt, LANES), lambda b, t, h: (row(b, t, h), 0)),
                  pl.BlockSpec((1, LANES), lambda b, t, h: (0, 0)),
                  pl.BlockSpec((1, LANES), lambda b, t, h: (0, 0)),
                  pl.BlockSpec((1, HEAD_DIM), lambda b, t, h: (0, 0))],
        out_specs=pl.BlockSpec((tt, hw), lambda b, t, h: (row(b, t, h), 0)),
        out_shape=jax.ShapeDtypeStruct((bsz * seq, hw), BF16),
        scratch_shapes=[pltpu.VMEM((N_HEADS, HEAD_DIM, HEAD_DIM), F32),
                        pltpu.VMEM((N_HEADS, nc, 2 * CHUNK, HEAD_DIM), BF16),
                        pltpu.VMEM((N_HEADS, nc, CHUNK, HEAD_DIM), F32),
                        pltpu.VMEM((N_HEADS, nc, CHUNK + HEAD_DIM, CHUNK), BF16),
                        pltpu.VMEM((N_HEADS, nc, 1, HEAD_DIM), F32),
                        pltpu.VMEM((tt, LANES), F32),
                        pltpu.VMEM((tt, LANES), F32)],
        compiler_params=_cparams("parallel", "arbitrary", "arbitrary"),
        name="delta",
    )(qkv, qkv, qkv, p, p_ba, avec, dtvec, nw)


def _sb_kernel(q_ref, k_ref, v_ref, o_ref, acc_scr, run_scr):
    blk = SB_QBLOCK
    n_chain = q_ref.shape[0] // blk
    chains = range(n_chain)
    first = pl.program_id(2) * n_chain
    scale = HEAD_DIM ** -0.5
    row = lax.broadcasted_iota(jnp.int32, (blk, blk), 0)
    col = lax.broadcasted_iota(jnp.int32, (blk, blk), 1)
    below = jnp.where(col < row, 1.0, 0.0)
    ones = jnp.ones((blk, blk), F32)
    r2 = lax.broadcasted_iota(jnp.int32, (blk, 2 * blk), 0)
    c2 = lax.broadcasted_iota(jnp.int32, (blk, 2 * blk), 1)
    cum_m = jnp.where((r2 > c2) | (c2 >= blk), 1.0, 0.0).astype(BF16)
    acc_scr[...] = jnp.zeros_like(acc_scr)
    run_scr[...] = jnp.zeros_like(run_scr)

    def body(carry):
        dist, _ = carry
        kbs = [first + c - dist for c in chains]
        ks = [pl.ds(pl.multiple_of(jnp.maximum(kb, 0) * blk, blk), blk) for kb in kbs]
        z = [_dot_nt(q_ref[c * blk:(c + 1) * blk, :], k_ref[ks[c], :]) * scale for c in chains]
        sp = [jnp.maximum(z[c], 0.0) + jnp.log(1.0 + jnp.exp(-jnp.abs(z[c]))) for c in chains]
        on_diag = jnp.where(dist > 0, ones, below)
        valid = [on_diag * jnp.where(kbs[c] >= 0, 1.0, 0.0) for c in chains]
        ls = [-sp[c] * valid[c] for c in chains]
        hi = [ls[c].astype(BF16) for c in chains]
        lo = [(ls[c] - hi[c].astype(F32)).astype(BF16) for c in chains]
        ct = [jnp.dot(hi[c], cum_m, preferred_element_type=F32)
              + jnp.dot(lo[c], cum_m, preferred_element_type=F32) for c in chains]
        live = jnp.full((blk, blk), -jnp.inf, F32)
        for c in chains:
            run = run_scr[c]
            a = jnp.exp((z[c] - sp[c]) + ct[c][:, :blk] + run) * valid[c]
            acc_scr[c] += _dot(a, v_ref[ks[c], :])
            run = run + ct[c][:, blk:]
            run_scr[c] = run
            live = jnp.maximum(live, jnp.where(kbs[c] >= 1, run, -jnp.inf))
        return dist + 1, jnp.max(live) >= EXP_UNDERFLOW

    lax.while_loop(lambda carry: carry[1], body, (jnp.int32(0), True))
    for c in chains:
        o_ref[c * blk:(c + 1) * blk, :] = acc_scr[c].astype(o_ref.dtype)


def _sb(p, bsz, seq):
    n_chain = min(8, seq // SB_QBLOCK)
    tq = n_chain * SB_QBLOCK
    nq = seq // tq
    cq = COL_SBQKV // HEAD_DIM
    return pl.pallas_call(
        _sb_kernel,
        grid=(bsz, N_HEADS, nq),
        in_specs=[pl.BlockSpec((tq, HEAD_DIM), lambda b, h, i: (b * nq + i, cq + h)),
                  pl.BlockSpec((seq, HEAD_DIM), lambda b, h, i: (b, cq + N_HEADS + h)),
                  pl.BlockSpec((seq, HEAD_DIM), lambda b, h, i: (b, cq + 2 * N_HEADS + h))],
        out_specs=pl.BlockSpec((tq, HEAD_DIM), lambda b, h, i: (b * nq + i, h)),
        out_shape=jax.ShapeDtypeStruct((bsz * seq, N_HEADS * HEAD_DIM), BF16),
        scratch_shapes=[pltpu.VMEM((n_chain, SB_QBLOCK, HEAD_DIM), F32),
                        pltpu.VMEM((n_chain, SB_QBLOCK, SB_QBLOCK), F32)],
        compiler_params=_cparams("parallel", "parallel", "arbitrary"),
        name="sb",
    )(p, p, p)


def _merge_kernel(oa_ref, ob_ref, ga_ref, gb_ref, wpa_ref, wpb_ref, o_ref):
    ya = jnp.dot(oa_ref[...], wpa_ref[...], preferred_element_type=F32)
    yb = jnp.dot(ob_ref[...], wpb_ref[...], preferred_element_type=F32)
    m = (jax.nn.sigmoid(ga_ref[...].astype(F32)) * ya
         + jax.nn.sigmoid(gb_ref[...].astype(F32)) * yb)
    o_ref[...] = m.astype(o_ref.dtype)


def _merge(o_a, o_b, p, w_pa, w_pb):
    n = o_a.shape[0]
    tm = 512
    hw = N_HEADS * HEAD_DIM
    return pl.pallas_call(
        _merge_kernel,
        grid=(n // tm,),
        in_specs=[pl.BlockSpec((tm, hw), lambda i: (i, 0)),
                  pl.BlockSpec((tm, hw), lambda i: (i, 0)),
                  pl.BlockSpec((tm, D_MODEL), lambda i: (i, COL_GATE_A // D_MODEL)),
                  pl.BlockSpec((tm, D_MODEL), lambda i: (i, COL_GATE_B // D_MODEL)),
                  pl.BlockSpec((hw, D_MODEL), lambda i: (0, 0)),
                  pl.BlockSpec((hw, D_MODEL), lambda i: (0, 0))],
        out_specs=pl.BlockSpec((tm, D_MODEL), lambda i: (i, 0)),
        out_shape=jax.ShapeDtypeStruct((n, D_MODEL), BF16),
        compiler_params=_cparams("parallel"),
        name="merge",
    )(o_a, o_b, p, p, w_pa, w_pb)


def _post1_kernel(alpha, m_ref, x_ref, wo_ref, g1p_ref, lg_ref, lb_ref, scp_ref, sh_ref,
                  x1_ref, ht_ref):
    y = jnp.dot(m_ref[...], wo_ref[...], preferred_element_type=F32)
    x1 = _layernorm(alpha * x_ref[...] + g1p_ref[0] * y) * lg_ref[...] + lb_ref[...]
    x1_ref[...] = x1
    h2 = _layernorm(x1) * scp_ref[0] + sh_ref[0]
    ht_ref[...] = h2.T.astype(ht_ref.dtype)


def _post1(merged, x2, w_o, g1p, ln_g, ln_b, scp2, sh2, seq, alpha):
    n = x2.shape[0]
    tm = min(512, seq)
    tpb = seq // tm
    bvec = lambda i: (i // tpb, 0, 0)
    return pl.pallas_call(
        functools.partial(_post1_kernel, alpha),
        grid=(n // tm,),
        in_specs=[pl.BlockSpec((tm, D_MODEL), lambda i: (i, 0)),
                  pl.BlockSpec((tm, D_MODEL), lambda i: (i, 0)),
                  pl.BlockSpec((D_MODEL, D_MODEL), lambda i: (0, 0)),
                  pl.BlockSpec((1, 1, D_MODEL), bvec),
                  pl.BlockSpec((1, D_MODEL), lambda i: (0, 0)),
                  pl.BlockSpec((1, D_MODEL), lambda i: (0, 0)),
                  pl.BlockSpec((1, 1, D_MODEL), bvec),
                  pl.BlockSpec((1, 1, D_MODEL), bvec)],
        out_specs=[pl.BlockSpec((tm, D_MODEL), lambda i: (i, 0)),
                   pl.BlockSpec((D_MODEL, tm), lambda i: (0, i))],
        out_shape=[jax.ShapeDtypeStruct((n, D_MODEL), F32),
                   jax.ShapeDtypeStruct((D_MODEL, n), BF16)],
        compiler_params=_cparams("parallel"),
        name="post1",
    )(merged, x2, w_o, g1p, ln_g, ln_b, scp2, sh2)


def _extract_top(lists, count, out_scrs, one_at_a_time):
    lists = list(lists)
    rids = [lax.broadcasted_iota(jnp.int32, s.shape, 0).astype(F32) for s in lists]
    for r in range(count):
        for i, s in enumerate(lists):
            m = jnp.max(s, axis=0, keepdims=True)
            if one_at_a_time:
                first = jnp.min(jnp.where(s == m, rids[i], float(s.shape[0])), axis=0, keepdims=True)
                lists[i] = jnp.where(rids[i] == first, -jnp.inf, s)
            else:
                lists[i] = jnp.where(s == m, -jnp.inf, s)
            out_scrs[i][r:r + 1, :] = m
    return lists


def _most_removed(lists):
    counts = [jnp.sum(jnp.where(s == -jnp.inf, 1.0, 0.0), axis=0, keepdims=True) for s in lists]
    return jnp.max(functools.reduce(jnp.maximum, counts))


N_CAND_ROWS = 80
N_CAND_PAD = 30


def _route_kernel(ht_ref, wqt_ref, keys_ref, s1_ref, e1_ref, s2_ref, e2_ref, tau_ref,
                  st_scr, v1_scr, v2_scr, cand_scr, top_scr):
    qt = jnp.dot(wqt_ref[...], ht_ref[...], preferred_element_type=F32)
    for hp in range(2 * N_HEADS):
        st_scr[hp] = jnp.dot(keys_ref[hp], qt[hp * HEAD_DIM:(hp + 1) * HEAD_DIM].astype(BF16),
                             preferred_element_type=F32)
    rid8 = lax.broadcasted_iota(jnp.int32, (8, ht_ref.shape[1]), 0)
    neg = -jnp.inf

    def head(h, carry):
        s1 = st_scr[2 * h]
        s2 = st_scr[2 * h + 1]
        left = _extract_top([s1, s2], PEER_TOPK, [v1_scr, v2_scr], False)

        @pl.when(_most_removed(left) > PEER_TOPK)
        def _():
            _extract_top([s1, s2], PEER_TOPK, [v1_scr, v2_scr], True)

        cand_scr[0:16, :] = v1_scr[0:1, :] + v2_scr[0:16, :]
        lo8 = v2_scr[0:8, :]
        for a in range(1, 8):
            nb = PEER_TOPK // (a + 1)
            cand_scr[8 + 8 * a:16 + 8 * a, :] = jnp.where(rid8 < nb, v1_scr[a:a + 1, :] + lo8, neg)
        cand_scr[72:80, :] = v1_scr[8:16, :] + v2_scr[0:1, :]
        cand = cand_scr[...]
        left = _extract_top([cand], PEER_TOPK, [top_scr], False)

        @pl.when(_most_removed(left) > PEER_TOPK + N_CAND_PAD)
        def _():
            _extract_top([cand], PEER_TOPK, [top_scr], True)

        tau = top_scr[PEER_TOPK - 1:PEER_TOPK, :]
        top = top_scr[0:1, :]
        z = jnp.sum(jnp.where(cand >= tau, jnp.exp(cand - top), 0.0), axis=0, keepdims=True)
        s1_ref[h] = s1
        s2_ref[h] = s2
        e1_ref[h] = jnp.exp(s1 - v1_scr[0:1, :])
        e2_ref[h] = jnp.exp(s2 - v2_scr[0:1, :]) * (1.0 / z)
        tau_ref[pl.ds(h, 1), :] = tau
        return carry

    lax.fori_loop(0, N_HEADS, head, 0)


def _route(ht, wqt, keys):
    n = ht.shape[1]
    tn = 2 * LANES
    tab = jax.ShapeDtypeStruct((N_HEADS, PEER_NKEYS, n), F32)
    tab_spec = pl.BlockSpec((N_HEADS, PEER_NKEYS, tn), lambda t: (0, 0, t))
    return pl.pallas_call(
        _route_kernel,
        grid=(n // tn,),
        in_specs=[pl.BlockSpec((D_MODEL, tn), lambda t: (0, t)),
                  pl.BlockSpec((D_MODEL, D_MODEL), lambda t: (0, 0)),
                  pl.BlockSpec((2 * N_HEADS, PEER_NKEYS, HEAD_DIM), lambda t: (0, 0, 0))],
        out_specs=[tab_spec, tab_spec, tab_spec, tab_spec,
                   pl.BlockSpec((N_HEADS, tn), lambda t: (0, t))],
        out_shape=[tab, tab, tab, tab, jax.ShapeDtypeStruct((N_HEADS, n), F32)],
        scratch_shapes=[pltpu.VMEM((2 * N_HEADS, PEER_NKEYS, tn), F32),
                        pltpu.VMEM((PEER_TOPK, tn), F32),
                        pltpu.VMEM((PEER_TOPK, tn), F32),
                        pltpu.VMEM((N_CAND_ROWS, tn), F32),
                        pltpu.VMEM((PEER_TOPK, tn), F32)],
        compiler_params=_cparams("parallel"),
        name="route",
    )(ht, wqt, keys)


EXPERTS_PER_STEP = 1024


def _gelu(x):
    return 0.5 * x * (1.0 + lax.erf(x * (2.0 ** -0.5)))


def _dense_kernel(n_chunks, ht_ref, u_ref, vt_ref, s1_ref, e1_ref, s2_ref, e2_ref, tau_ref, yt_ref,
                  act0_scr, act1_scr, wa0_scr, wa1_scr):
    tn = ht_ref.shape[1]
    g = pl.program_id(0)

    @pl.when(g == 0)
    def _():
        for ref in (act0_scr, act1_scr, wa0_scr, wa1_scr):
            ref[...] = jnp.zeros_like(ref)

    @pl.when(jnp.logical_or(g == 0, (g + 2 * n_chunks - 2) % n_chunks == 0))
    def _():
        yt_ref[...] = jnp.zeros_like(yt_ref)

    n_first = EXPERTS_PER_STEP // PEER_NKEYS
    j_rows = 16
    j_sub = j_rows // 8
    pieces = 4
    j_per_piece = PEER_NKEYS // j_rows // pieces
    a_rows = EXPERTS_PER_STEP // pieces
    c_rows = D_MODEL // pieces

    def piece(act_a, act_b, wa_b, wa_c, jq, carry):
        rows_a = slice(jq * a_rows, (jq + 1) * a_rows)
        rows_c = slice(jq * c_rows, (jq + 1) * c_rows)
        half = tn // 2

        def gate_block(js, lt):
            j0 = (jq * j_per_piece + js) * j_rows
            cols = slice(lt * LANES, (lt + 1) * LANES)
            w = [jnp.zeros((j_sub, 8, LANES), F32) for _ in range(n_first)]
            for h in range(N_HEADS):
                s2q = s2_ref[h, j0:j0 + j_rows, cols].reshape(j_sub, 8, LANES)
                e2q = e2_ref[h, j0:j0 + j_rows, cols].reshape(j_sub, 8, LANES)
                tau = jnp.broadcast_to(tau_ref[h:h + 1, cols], (8, LANES))
                for ii in range(n_first):
                    s1b = jnp.broadcast_to(s1_ref[h, ii:ii + 1, cols], (8, LANES))
                    e1b = jnp.broadcast_to(e1_ref[h, ii:ii + 1, cols], (8, LANES))
                    w[ii] = w[ii] + jnp.where(s1b + s2q >= tau, e1b * e2q, 0.0)
            for ii in range(n_first):
                er = slice(ii * PEER_NKEYS + j0, ii * PEER_NKEYS + j0 + j_rows)
                wa_b[er, cols] = (w[ii].reshape(j_rows, LANES) * act_b[er, cols]).astype(BF16)

        blocks = [(js, lt) for js in range(j_per_piece) for lt in range(tn // LANES)]
        quarter = len(blocks) // 4
        for part in range(4):
            if part < 2:
                cs = slice(part * half, (part + 1) * half)
                res = jnp.dot(u_ref[rows_a, :], ht_ref[:, cs], preferred_element_type=F32)
                act_a[rows_a, cs] = _gelu(res)
            else:
                cs = slice((part - 2) * half, (part - 1) * half)
                res = jnp.dot(vt_ref[rows_c, :], wa_c[:, cs], preferred_element_type=F32)
                yt_ref[rows_c, cs] += res
            for js, lt in blocks[part * quarter:(part + 1) * quarter]:
                gate_block(js, lt)
        return carry

    @pl.when(g % 2 == 0)
    def _():
        for jq in range(pieces):
            piece(act0_scr, act1_scr, wa1_scr, wa0_scr, jq, 0)

    @pl.when(g % 2 == 1)
    def _():
        for jq in range(pieces):
            piece(act1_scr, act0_scr, wa0_scr, wa1_scr, jq, 0)


def _dense(ht, u_tab, vt_tab, s1, e1, s2, e2, tau):
    n = ht.shape[1]
    tn = 512
    n_chunks = u_tab.shape[0] // EXPERTS_PER_STEP
    total = (n // tn) * n_chunks
    rows_per_step = EXPERTS_PER_STEP // PEER_NKEYS
    job_a = lambda g: jnp.minimum(g, total - 1)
    job_b = lambda g: jnp.clip(g - 1, 0, total - 1)
    job_c = lambda g: jnp.clip(g - 2, 0, total - 1)
    row_spec = pl.BlockSpec((N_HEADS, rows_per_step, tn),
                            lambda g: (0, job_b(g) % n_chunks, job_b(g) // n_chunks))
    tab_spec = pl.BlockSpec((N_HEADS, PEER_NKEYS, tn), lambda g: (0, 0, job_b(g) // n_chunks))
    return pl.pallas_call(
        functools.partial(_dense_kernel, n_chunks),
        grid=(total + 2,),
        in_specs=[pl.BlockSpec((D_MODEL, tn), lambda g: (0, job_a(g) // n_chunks)),
                  pl.BlockSpec((EXPERTS_PER_STEP, D_MODEL), lambda g: (job_a(g) % n_chunks, 0)),
                  pl.BlockSpec((D_MODEL, EXPERTS_PER_STEP), lambda g: (0, job_c(g) % n_chunks)),
                  row_spec, row_spec, tab_spec, tab_spec,
                  pl.BlockSpec((N_HEADS, tn), lambda g: (0, job_b(g) // n_chunks))],
        out_specs=pl.BlockSpec((D_MODEL, tn), lambda g: (0, job_c(g) // n_chunks)),
        out_shape=jax.ShapeDtypeStruct((D_MODEL, n), F32),
        scratch_shapes=[pltpu.VMEM((EXPERTS_PER_STEP, tn), F32),
                        pltpu.VMEM((EXPERTS_PER_STEP, tn), F32),
                        pltpu.VMEM((EXPERTS_PER_STEP, tn), BF16),
                        pltpu.VMEM((EXPERTS_PER_STEP, tn), BF16)],
        compiler_params=_cparams("arbitrary"),
        name="dense",
    )(ht, u_tab, vt_tab, s1, e1, s2, e2, tau)


def _post2_kernel(alpha, yt_ref, x_ref, g2p_ref, lg_ref, lb_ref, o_ref):
    y = yt_ref[...].T
    o_ref[...] = _layernorm(alpha * x_ref[...] + g2p_ref[0] * y) * lg_ref[...] + lb_ref[...]


def _post2(yt, x1, g2p, ln_g, ln_b, seq, alpha):
    n = x1.shape[0]
    tm = min(512, seq)
    tpb = seq // tm
    return pl.pallas_call(
        functools.partial(_post2_kernel, alpha),
        grid=(n // tm,),
        in_specs=[pl.BlockSpec((D_MODEL, tm), lambda i: (0, i)),
                  pl.BlockSpec((tm, D_MODEL), lambda i: (i, 0)),
                  pl.BlockSpec((1, 1, D_MODEL), lambda i: (i // tpb, 0, 0)),
                  pl.BlockSpec((1, D_MODEL), lambda i: (0, 0)),
                  pl.BlockSpec((1, D_MODEL), lambda i: (0, 0))],
        out_specs=pl.BlockSpec((tm, D_MODEL), lambda i: (i, 0)),
        out_shape=jax.ShapeDtypeStruct((n, D_MODEL), F32),
        compiler_params=_cparams("parallel"),
        name="post2",
    )(yt, x1, g2p, ln_g, ln_b)


def _layer(x2, c, bsz, seq, alpha, w_ada, b_ada, w_in, conv_w, a_log, dt_bias, dn_norm_w, w_pa,
           w_pb, w_o, ln1_g, ln1_b, peer_wq, peer_keys, peer_u, peer_v, ln2_g, ln2_b):
    mod = _ada(c, w_ada, b_ada).reshape(bsz, 6, 1, D_MODEL)
    sh1, sc1, g1, sh2, sc2, g2 = (mod[:, i] for i in range(6))

    o_b, o_a, o_g, o_sb, o_ga = 3072, 3080, 3088, 4112, 7184
    w_main = jnp.concatenate([w_in[:, :o_b], w_in[:, o_sb:o_ga], w_in[:, o_ga:], w_in[:, o_g:o_sb]],
                             axis=1).astype(BF16)
    w_ba = jnp.zeros((D_MODEL, LANES), F32).at[:, :2 * N_HEADS].set(w_in[:, o_b:o_g]).astype(BF16)
    avec = jnp.zeros((1, LANES), F32).at[0, N_HEADS:2 * N_HEADS].set(a_log)
    dtvec = jnp.zeros((1, LANES), F32).at[0, N_HEADS:2 * N_HEADS].set(dt_bias)

    p, p_ba = _inproj(x2, 1.0 + sc1, sh1, w_main, w_ba, seq)
    qkv = _conv(p, conv_w, bsz, seq)
    o_a = _delta(qkv, p, p_ba, avec, dtvec, dn_norm_w.reshape(1, HEAD_DIM), bsz, seq)
    o_b = _sb(p, bsz, seq)
    merged = _merge(o_a, o_b, p, w_pa.astype(BF16), w_pb.astype(BF16))
    x1, ht = _post1(merged, x2, w_o.astype(BF16), 1.0 + g1, ln1_g.reshape(1, D_MODEL),
                    ln1_b.reshape(1, D_MODEL), 1.0 + sc2, sh2, seq, alpha)

    keys = peer_keys.reshape(2 * N_HEADS, PEER_NKEYS, HEAD_DIM).astype(BF16)
    s1, e1, s2, e2, tau = _route(ht, peer_wq.T.astype(BF16), keys)
    yt = _dense(ht, peer_u.astype(BF16), peer_v.T.astype(BF16), s1, e1, s2, e2, tau)
    return _post2(yt, x1, 1.0 + g2, ln2_g.reshape(1, D_MODEL), ln2_b.reshape(1, D_MODEL), seq, alpha)


def kernel(x, c, w_ada, b_ada, w_in, conv_w, a_log, dt_bias, dn_norm_w, w_pa, w_pb, w_o, ln1_g, ln1_b,
           peer_wq, peer_keys, peer_u, peer_v, ln2_g, ln2_b):
    bsz, seq, _ = x.shape
    depth = w_ada.shape[0]
    alpha = (2 * depth) ** 0.25
    x2 = x.reshape(bsz * seq, D_MODEL)
    for l in range(depth):
        x2 = _layer(x2, c, bsz, seq, alpha, w_ada[l], b_ada[l], w_in[l], conv_w[l], a_log[l], dt_bias[l],
                    dn_norm_w[l], w_pa[l], w_pb[l], w_o[l], ln1_g[l], ln1_b[l], peer_wq[l], peer_keys[l],
                    peer_u[l], peer_v[l], ln2_g[l], ln2_b[l])
    return x2.reshape(bsz, seq, D_MODEL)
```

```python
import functools
import math

import jax
import jax.numpy as jnp
from jax import lax
from jax.experimental import pallas as pl
from jax.experimental.pallas import tpu as pltpu

F32 = jnp.float32
BF16 = jnp.bfloat16
HIGHEST = lax.Precision.HIGHEST

LN_EPS = 1e-5
NORM_EPS = 1e-6
D_MODEL = 2048
LANES = 128
CHUNK = 64
N_HEADS = 8
HEAD_DIM = 128
DN_CONV = 4
QKV_W = 3 * N_HEADS * HEAD_DIM
SB_QBLOCK = 128
PEER_NKEYS = 128
PEER_TOPK = 16
EXP_UNDERFLOW = -104.0
VMEM_LIMIT = 56 * 1024 * 1024

COL_DNQKV = 0
COL_SBQKV = 3072
COL_GATE_A = 6144
COL_GATE_B = 8192
COL_DNG = 10240
MAIN_W = 11264


def _cparams(*sem):
    return pltpu.CompilerParams(dimension_semantics=sem, vmem_limit_bytes=VMEM_LIMIT)


def _dot(a, b):
    return jnp.dot(a.astype(BF16), b.astype(BF16), preferred_element_type=F32)


def _dot_nt(a, b):
    return lax.dot_general(a.astype(BF16), b.astype(BF16), (((1,), (1,)), ((), ())),
                           preferred_element_type=F32)


def _silu(x):
    return x * jax.nn.sigmoid(x)


def _softplus(x):
    return jnp.maximum(x, 0.0) + jnp.log1p(jnp.exp(-jnp.abs(x)))


def _layernorm(x):
    mu = jnp.mean(x, axis=-1, keepdims=True)
    xc = x - mu
    var = jnp.mean(xc * xc, axis=-1, keepdims=True)
    return xc * lax.rsqrt(var + LN_EPS)


def _ada_kernel(c_ref, w_ref, b_ref, o_ref):
    s = _silu(c_ref[...])
    o_ref[...] = jnp.dot(s, w_ref[...], preferred_element_type=F32, precision=HIGHEST) + b_ref[...]


def _ada(c, w_ada, b_ada):
    bsz = c.shape[0]
    width = w_ada.shape[1]
    tn = 1024
    cp = jnp.zeros((8, D_MODEL), F32).at[:bsz].set(c)
    out = pl.pallas_call(
        _ada_kernel,
        grid=(width // tn,),
        in_specs=[pl.BlockSpec((8, D_MODEL), lambda j: (0, 0)),
                  pl.BlockSpec((D_MODEL, tn), lambda j: (0, j)),
                  pl.BlockSpec((1, tn), lambda j: (0, j))],
        out_specs=pl.BlockSpec((8, tn), lambda j: (0, j)),
        out_shape=jax.ShapeDtypeStruct((8, width), F32),
        compiler_params=_cparams("parallel"),
        name="ada",
    )(cp, w_ada, b_ada.reshape(1, width))
    return out[:bsz]


def _inproj_kernel(x_ref, scp_ref, sh_ref, w_ref, wba_ref, p_ref, pba_ref, h_scr):
    @pl.when(pl.program_id(1) == 0)
    def _():
        h = (_layernorm(x_ref[...]) * scp_ref[0] + sh_ref[0]).astype(BF16)
        h_scr[...] = h
        pba_ref[...] = jnp.dot(h, wba_ref[...], preferred_element_type=F32)

    p_ref[...] = jnp.dot(h_scr[...], w_ref[...], preferred_element_type=F32).astype(p_ref.dtype)


def _inproj(x2, scp, sh, w_main, w_ba, seq):
    n = x2.shape[0]
    tm = min(1024, seq)
    tn = 1024
    tpb = seq // tm
    return pl.pallas_call(
        _inproj_kernel,
        grid=(n // tm, MAIN_W // tn),
        in_specs=[pl.BlockSpec((tm, D_MODEL), lambda i, j: (i, 0)),
                  pl.BlockSpec((1, 1, D_MODEL), lambda i, j: (i // tpb, 0, 0)),
                  pl.BlockSpec((1, 1, D_MODEL), lambda i, j: (i // tpb, 0, 0)),
                  pl.BlockSpec((D_MODEL, tn), lambda i, j: (0, j)),
                  pl.BlockSpec((D_MODEL, LANES), lambda i, j: (0, 0))],
        out_specs=[pl.BlockSpec((tm, tn), lambda i, j: (i, j)),
                   pl.BlockSpec((tm, LANES), lambda i, j: (i, 0))],
        out_shape=[jax.ShapeDtypeStruct((n, MAIN_W), BF16),
                   jax.ShapeDtypeStruct((n, LANES), F32)],
        scratch_shapes=[pltpu.VMEM((tm, D_MODEL), BF16)],
        compiler_params=_cparams("parallel", "arbitrary"),
        name="inproj",
    )(x2, scp, sh, w_main, w_ba)


def _conv_kernel(x_ref, w_ref, o_ref, prev_scr):
    tt = x_ref.shape[0]

    @pl.when(pl.program_id(1) == 0)
    def _():
        prev_scr[...] = jnp.zeros_like(prev_scr)

    for g in range(QKV_W // LANES):
        cols = slice(g * LANES, (g + 1) * LANES)
        x = x_ref[:, cols].astype(F32)
        xp = jnp.concatenate([prev_scr[:, cols], x], axis=0)
        w = w_ref[:, cols]
        y = w[DN_CONV - 1:DN_CONV] * x
        for i in range(DN_CONV - 1):
            off = 8 - (DN_CONV - 1) + i
            y = y + w[i:i + 1] * xp[off:off + tt]
        prev_scr[:, cols] = x[tt - 8:]
        y = _silu(y)
        if g < 2 * N_HEADS:
            y = y * lax.rsqrt(jnp.sum(y * y, axis=-1, keepdims=True) + NORM_EPS)
        o_ref[:, cols] = y.astype(o_ref.dtype)


def _conv(p, conv_w, bsz, seq):
    tt = min(256, seq)
    tpb = seq // tt
    return pl.pallas_call(
        _conv_kernel,
        grid=(bsz, tpb),
        in_specs=[pl.BlockSpec((tt, QKV_W), lambda b, t: (b * tpb + t, COL_DNQKV // QKV_W)),
                  pl.BlockSpec((DN_CONV, QKV_W), lambda b, t: (0, 0))],
        out_specs=pl.BlockSpec((tt, QKV_W), lambda b, t: (b * tpb + t, 0)),
        out_shape=jax.ShapeDtypeStruct((bsz * seq, QKV_W), BF16),
        scratch_shapes=[pltpu.VMEM((8, QKV_W), F32)],
        compiler_params=_cparams("parallel", "arbitrary"),
        name="conv",
    )(p, conv_w)


def _bdot(a, b):
    return lax.dot_general(a.astype(BF16), b.astype(BF16), (((2,), (1,)), ((0,), (0,))),
                           preferred_element_type=F32)


def _bdot_nt(a, b):
    return lax.dot_general(a.astype(BF16), b.astype(BF16), (((2,), (2,)), ((0,), (0,))),
                           preferred_element_type=F32)


def _bdot_exact(a, b):
    return lax.dot_general(a, b, (((2,), (1,)), ((0,), (0,))), preferred_element_type=F32,
                           precision=HIGHEST)


def _delta_kernel(q_ref, k_ref, v_ref, gate_ref, ba_ref, avec_ref, dtvec_ref, nw_ref, o_ref,
                  s_scr, wq_scr, u_scr, akd_scr, cd_scr, beta_scr, gc_scr):
    tt = q_ref.shape[0]
    nc = tt // CHUNK
    h = pl.program_id(2)

    @pl.when(jnp.logical_and(pl.program_id(1) == 0, h == 0))
    def _():
        s_scr[...] = jnp.zeros_like(s_scr)

    r = lax.broadcasted_iota(jnp.int32, (nc, CHUNK, CHUNK), 1)
    c = lax.broadcasted_iota(jnp.int32, (nc, CHUNK, CHUNK), 2)
    incl = r >= c
    strict = r > c
    eye = (r == c).astype(F32)

    @pl.when(h == 0)
    def _():
        ba = ba_ref[...]
        beta_scr[...] = jax.nn.sigmoid(ba)
        g_all = -jnp.exp(avec_ref[...]) * _softplus(ba + dtvec_ref[...])
        gc_all = _bdot_exact(incl.astype(F32), g_all.reshape(nc, CHUNK, LANES))
        gc_scr[...] = gc_all.reshape(tt, LANES)

    lane = lax.broadcasted_iota(jnp.int32, (tt, LANES), 1)
    b_col = jnp.sum(jnp.where(lane == h, beta_scr[...], 0.0), axis=1, keepdims=True)
    gc_col = jnp.sum(jnp.where(lane == h + N_HEADS, gc_scr[...], 0.0), axis=1, keepdims=True)
    beta = jnp.broadcast_to(b_col, (tt, LANES)).reshape(nc, CHUNK, LANES)
    gc = jnp.broadcast_to(gc_col, (tt, LANES)).reshape(nc, CHUNK, LANES)
    gc_rows = jnp.stack([gc[ci].T[:CHUNK] for ci in range(nc)])

    q = q_ref[...].astype(F32).reshape(nc, CHUNK, HEAD_DIM) * (HEAD_DIM ** -0.5)
    k = k_ref[...].astype(F32).reshape(nc, CHUNK, HEAD_DIM)
    v = v_ref[...].astype(F32).reshape(nc, CHUNK, HEAD_DIM)
    decay = jnp.where(incl, jnp.exp(gc[:, :, :CHUNK] - gc_rows), 0.0)
    egc = jnp.exp(gc)
    gl = gc[:, CHUNK - 1:CHUNK, :]
    kb = k * beta
    p = jnp.where(strict, -(_bdot_nt(kb, k) * decay), 0.0)
    tm = eye + p
    for _ in range(CHUNK.bit_length() - 2):
        p = _bdot(p, p)
        tm = tm + _bdot(tm, p)
    wq_scr[h, :, :CHUNK, :] = _bdot(tm, kb * egc).astype(BF16)
    wq_scr[h, :, CHUNK:, :] = (q * egc).astype(BF16)
    u_scr[h] = _bdot(tm, v * beta)
    akd_scr[h, :, :CHUNK, :] = (_bdot_nt(q, k) * decay).astype(BF16)
    kd = k * jnp.exp(gl - gc)
    for ci in range(nc):
        akd_scr[h, ci, CHUNK:, :] = kd[ci].T.astype(BF16)
    cd_scr[h] = jnp.exp(gl)

    @pl.when(h == N_HEADS - 1)
    def _():
        def chunk(ci, carry):
            rows = pl.ds(pl.multiple_of(ci * CHUNK, CHUNK), CHUNK)
            heads = range(N_HEADS)
            ws = [jnp.dot(wq_scr[hh, ci], s_scr[hh].astype(BF16), preferred_element_type=F32)
                  for hh in heads]
            v_new = [(u_scr[hh, ci] - ws[hh][:CHUNK]).astype(BF16) for hh in heads]
            av = [jnp.dot(akd_scr[hh, ci], v_new[hh], preferred_element_type=F32) for hh in heads]
            for hh in heads:
                s_scr[hh] = s_scr[hh] * cd_scr[hh, ci] + av[hh][CHUNK:]
            for hh in heads:
                cols = slice(hh * HEAD_DIM, (hh + 1) * HEAD_DIM)
                o = ws[hh][CHUNK:] + av[hh][:CHUNK]
                on = o * lax.rsqrt(jnp.mean(o * o, axis=-1, keepdims=True) + NORM_EPS) * nw_ref[...]
                o_ref[rows, cols] = (on * _silu(gate_ref[rows, cols].astype(F32))).astype(o_ref.dtype)
            return carry

        lax.fori_loop(0, nc, chunk, 0)


def _delta(qkv, p, p_ba, avec, dtvec, nw, bsz, seq):
    tt = min(1024, seq)
    nc = tt // CHUNK
    tpb = seq // tt
    hw = N_HEADS * HEAD_DIM
    row = lambda b, t, h: b * tpb + t
    return pl.pallas_call(
        _delta_kernel,
        grid=(bsz, tpb, N_HEADS),
        in_specs=[pl.BlockSpec((tt, HEAD_DIM), lambda b, t, h: (row(b, t, h), h)),
                  pl.BlockSpec((tt, HEAD_DIM), lambda b, t, h: (row(b, t, h), N_HEADS + h)),
                  pl.BlockSpec((tt, HEAD_DIM), lambda b, t, h: (row(b, t, h), 2 * N_HEADS + h)),
                  pl.BlockSpec((tt, hw), lambda b, t, h: (row(b, t, h), COL_DNG // hw)),
                  pl.BlockSpec((tt, LANES), lambda b, t, h: (row(b, t, h), 0)),
                  pl.BlockSpec((1, LANES), lambda b, t, h: (0, 0)),
                  pl.BlockSpec((1, LANES), lambda b, t, h: (0, 0)),
                  pl.BlockSpec((1, HEAD_DIM), lambda b, t, h: (0, 0))],
        out_specs=pl.BlockSpec((tt, hw), lambda b, t, h: (row(b, t, h), 0)),
        out_shape=jax.ShapeDtypeStruct((bsz * seq, hw), BF16),
        scratch_shapes=[pltpu.VMEM((N_HEADS, HEAD_DIM, HEAD_DIM), F32),
                        pltpu.VMEM((N_HEADS, nc, 2 * CHUNK, HEAD_DIM), BF16),
                        pltpu.VMEM((N_HEADS, nc, CHUNK, HEAD_DIM), F32),
                        pltpu.VMEM((N_HEADS, nc, CHUNK + HEAD_DIM, CHUNK), BF16),
                        pltpu.VMEM((N_HEADS, nc, 1, HEAD_DIM), F32),
                        pltpu.VMEM((tt, LANES), F32),
                        pltpu.VMEM((tt, LANES), F32)],
        compiler_params=_cparams("parallel", "arbitrary", "arbitrary"),
        name="delta",
    )(qkv, qkv, qkv, p, p_ba, avec, dtvec, nw)


def _sb_kernel(q_ref, k_ref, v_ref, o_ref, acc_scr, run_scr):
    blk = SB_QBLOCK
    n_chain = q_ref.shape[0] // blk
    chains = range(n_chain)
    first = pl.program_id(2) * n_chain
    scale = HEAD_DIM ** -0.5
    row = lax.broadcasted_iota(jnp.int32, (blk, blk), 0)
    col = lax.broadcasted_iota(jnp.int32, (blk, blk), 1)
    below = jnp.where(col < row, 1.0, 0.0)
    ones = jnp.ones((blk, blk), F32)
    r2 = lax.broadcasted_iota(jnp.int32, (blk, 2 * blk), 0)
    c2 = lax.broadcasted_iota(jnp.int32, (blk, 2 * blk), 1)
    cum_m = jnp.where((r2 > c2) | (c2 >= blk), 1.0, 0.0).astype(BF16)
    acc_scr[...] = jnp.zeros_like(acc_scr)
    run_scr[...] = jnp.zeros_like(run_scr)

    def body(carry):
        dist, _ = carry
        kbs = [first + c - dist for c in chains]
        ks = [pl.ds(pl.multiple_of(jnp.maximum(kb, 0) * blk, blk), blk) for kb in kbs]
        z = [_dot_nt(q_ref[c * blk:(c + 1) * blk, :], k_ref[ks[c], :]) * scale for c in chains]
        sp = [jnp.maximum(z[c], 0.0) + jnp.log(1.0 + jnp.exp(-jnp.abs(z[c]))) for c in chains]
        on_diag = jnp.where(dist > 0, ones, below)
        valid = [on_diag * jnp.where(kbs[c] >= 0, 1.0, 0.0) for c in chains]
        ls = [-sp[c] * valid[c] for c in chains]
        hi = [ls[c].astype(BF16) for c in chains]
        lo = [(ls[c] - hi[c].astype(F32)).astype(BF16) for c in chains]
        ct = [jnp.dot(hi[c], cum_m, preferred_element_type=F32)
              + jnp.dot(lo[c], cum_m, preferred_element_type=F32) for c in chains]
        live = jnp.full((blk, blk), -jnp.inf, F32)
        for c in chains:
            run = run_scr[c]
            a = jnp.exp((z[c] - sp[c]) + ct[c][:, :blk] + run) * valid[c]
            acc_scr[c] += _dot(a, v_ref[ks[c], :])
            run = run + ct[c][:, blk:]
            run_scr[c] = run
            live = jnp.maximum(live, jnp.where(kbs[c] >= 1, run, -jnp.inf))
        return dist + 1, jnp.max(live) >= EXP_UNDERFLOW

    lax.while_loop(lambda carry: carry[1], body, (jnp.int32(0), True))
    for c in chains:
        o_ref[c * blk:(c + 1) * blk, :] = acc_scr[c].astype(o_ref.dtype)


def _sb(p, bsz, seq):
    n_chain = min(8, seq // SB_QBLOCK)
    tq = n_chain * SB_QBLOCK
    nq = seq // tq
    cq = COL_SBQKV // HEAD_DIM
    return pl.pallas_call(
        _sb_kernel,
        grid=(bsz, N_HEADS, nq),
        in_specs=[pl.BlockSpec((tq, HEAD_DIM), lambda b, h, i: (b * nq + i, cq + h)),
                  pl.BlockSpec((seq, HEAD_DIM), lambda b, h, i: (b, cq + N_HEADS + h)),
                  pl.BlockSpec((seq, HEAD_DIM), lambda b, h, i: (b, cq + 2 * N_HEADS + h))],
        out_specs=pl.BlockSpec((tq, HEAD_DIM), lambda b, h, i: (b * nq + i, h)),
        out_shape=jax.ShapeDtypeStruct((bsz * seq, N_HEADS * HEAD_DIM), BF16),
        scratch_shapes=[pltpu.VMEM((n_chain, SB_QBLOCK, HEAD_DIM), F32),
                        pltpu.VMEM((n_chain, SB_QBLOCK, SB_QBLOCK), F32)],
        compiler_params=_cparams("parallel", "parallel", "arbitrary"),
        name="sb",
    )(p, p, p)


def _merge_kernel(oa_ref, ob_ref, ga_ref, gb_ref, wpa_ref, wpb_ref, o_ref):
    ya = jnp.dot(oa_ref[...], wpa_ref[...], preferred_element_type=F32)
    yb = jnp.dot(ob_ref[...], wpb_ref[...], preferred_element_type=F32)
    m = (jax.nn.sigmoid(ga_ref[...].astype(F32)) * ya
         + jax.nn.sigmoid(gb_ref[...].astype(F32)) * yb)
    o_ref[...] = m.astype(o_ref.dtype)


def _merge(o_a, o_b, p, w_pa, w_pb):
    n = o_a.shape[0]
    tm = 512
    hw = N_HEADS * HEAD_DIM
    return pl.pallas_call(
        _merge_kernel,
        grid=(n // tm,),
        in_specs=[pl.BlockSpec((tm, hw), lambda i: (i, 0)),
                  pl.BlockSpec((tm, hw), lambda i: (i, 0)),
                  pl.BlockSpec((tm, D_MODEL), lambda i: (i, COL_GATE_A // D_MODEL)),
                  pl.BlockSpec((tm, D_MODEL), lambda i: (i, COL_GATE_B // D_MODEL)),
                  pl.BlockSpec((hw, D_MODEL), lambda i: (0, 0)),
                  pl.BlockSpec((hw, D_MODEL), lambda i: (0, 0))],
        out_specs=pl.BlockSpec((tm, D_MODEL), lambda i: (i, 0)),
        out_shape=jax.ShapeDtypeStruct((n, D_MODEL), BF16),
        compiler_params=_cparams("parallel"),
        name="merge",
    )(o_a, o_b, p, p, w_pa, w_pb)


def _post1_kernel(alpha, m_ref, x_ref, wo_ref, g1p_ref, lg_ref, lb_ref, scp_ref, sh_ref,
                  x1_ref, ht_ref):
    y = jnp.dot(m_ref[...], wo_ref[...], preferred_element_type=F32)
    x1 = _layernorm(alpha * x_ref[...] + g1p_ref[0] * y) * lg_ref[...] + lb_ref[...]
    x1_ref[...] = x1
    h2 = _layernorm(x1) * scp_ref[0] + sh_ref[0]
    ht_ref[...] = h2.T.astype(ht_ref.dtype)


def _post1(merged, x2, w_o, g1p, ln_g, ln_b, scp2, sh2, seq, alpha):
    n = x2.shape[0]
    tm = min(512, seq)
    tpb = seq // tm
    bvec = lambda i: (i // tpb, 0, 0)
    return pl.pallas_call(
        functools.partial(_post1_kernel, alpha),
        grid=(n // tm,),
        in_specs=[pl.BlockSpec((tm, D_MODEL), lambda i: (i, 0)),
                  pl.BlockSpec((tm, D_MODEL), lambda i: (i, 0)),
                  pl.BlockSpec((D_MODEL, D_MODEL), lambda i: (0, 0)),
                  pl.BlockSpec((1, 1, D_MODEL), bvec),
                  pl.BlockSpec((1, D_MODEL), lambda i: (0, 0)),
                  pl.BlockSpec((1, D_MODEL), lambda i: (0, 0)),
                  pl.BlockSpec((1, 1, D_MODEL), bvec),
                  pl.BlockSpec((1, 1, D_MODEL), bvec)],
        out_specs=[pl.BlockSpec((tm, D_MODEL), lambda i: (i, 0)),
                   pl.BlockSpec((D_MODEL, tm), lambda i: (0, i))],
        out_shape=[jax.ShapeDtypeStruct((n, D_MODEL), F32),
                   jax.ShapeDtypeStruct((D_MODEL, n), BF16)],
        compiler_params=_cparams("parallel"),
        name="post1",
    )(merged, x2, w_o, g1p, ln_g, ln_b, scp2, sh2)


def _extract_top(lists, count, out_scrs, one_at_a_time):
    lists = list(lists)
    rids = [lax.broadcasted_iota(jnp.int32, s.shape, 0).astype(F32) for s in lists]
    for r in range(count):
        for i, s in enumerate(lists):
            m = jnp.max(s, axis=0, keepdims=True)
            if one_at_a_time:
                first = jnp.min(jnp.where(s == m, rids[i], float(s.shape[0])), axis=0, keepdims=True)
                lists[i] = jnp.where(rids[i] == first, -jnp.inf, s)
            else:
                lists[i] = jnp.where(s == m, -jnp.inf, s)
            out_scrs[i][r:r + 1, :] = m
    return lists


def _most_removed(lists):
    counts = [jnp.sum(jnp.where(s == -jnp.inf, 1.0, 0.0), axis=0, keepdims=True) for s in lists]
    return jnp.max(functools.reduce(jnp.maximum, counts))


N_CAND_ROWS = 80
N_CAND_PAD = 30


def _route_kernel(ht_ref, wqt_ref, keys_ref, s1_ref, e1_ref, s2_ref, e2_ref, tau_ref,
                  st_scr, v1_scr, v2_scr, cand_scr, top_scr):
    qt = jnp.dot(wqt_ref[...], ht_ref[...], preferred_element_type=F32)
    for hp in range(2 * N_HEADS):
        st_scr[hp] = jnp.dot(keys_ref[hp], qt[hp * HEAD_DIM:(hp + 1) * HEAD_DIM].astype(BF16),
                             preferred_element_type=F32)
    rid8 = lax.broadcasted_iota(jnp.int32, (8, ht_ref.shape[1]), 0)
    neg = -jnp.inf

    def head(h, carry):
        s1 = st_scr[2 * h]
        s2 = st_scr[2 * h + 1]
        left = _extract_top([s1, s2], PEER_TOPK, [v1_scr, v2_scr], False)

        @pl.when(_most_removed(left) > PEER_TOPK)
        def _():
            _extract_top([s1, s2], PEER_TOPK, [v1_scr, v2_scr], True)

        cand_scr[0:16, :] = v1_scr[0:1, :] + v2_scr[0:16, :]
        lo8 = v2_scr[0:8, :]
        for a in range(1, 8):
            nb = PEER_TOPK // (a + 1)
            cand_scr[8 + 8 * a:16 + 8 * a, :] = jnp.where(rid8 < nb, v1_scr[a:a + 1, :] + lo8, neg)
        cand_scr[72:80, :] = v1_scr[8:16, :] + v2_scr[0:1, :]
        cand = cand_scr[...]
        left = _extract_top([cand], PEER_TOPK, [top_scr], False)

        @pl.when(_most_removed(left) > PEER_TOPK + N_CAND_PAD)
        def _():
            _extract_top([cand], PEER_TOPK, [top_scr], True)

        tau = top_scr[PEER_TOPK - 1:PEER_TOPK, :]
        top = top_scr[0:1, :]
        z = jnp.sum(jnp.where(cand >= tau, jnp.exp(cand - top), 0.0), axis=0, keepdims=True)
        s1_ref[h] = s1
        s2_ref[h] = s2
        e1_ref[h] = jnp.exp(s1 - v1_scr[0:1, :])
        e2_ref[h] = jnp.exp(s2 - v2_scr[0:1, :]) * (1.0 / z)
        tau_ref[pl.ds(h, 1), :] = tau
        return carry

    lax.fori_loop(0, N_HEADS, head, 0)


def _route(ht, wqt, keys):
    n = ht.shape[1]
    tn = 2 * LANES
    tab = jax.ShapeDtypeStruct((N_HEADS, PEER_NKEYS, n), F32)
    tab_spec = pl.BlockSpec((N_HEADS, PEER_NKEYS, tn), lambda t: (0, 0, t))
    return pl.pallas_call(
        _route_kernel,
        grid=(n // tn,),
        in_specs=[pl.BlockSpec((D_MODEL, tn), lambda t: (0, t)),
                  pl.BlockSpec((D_MODEL, D_MODEL), lambda t: (0, 0)),
                  pl.BlockSpec((2 * N_HEADS, PEER_NKEYS, HEAD_DIM), lambda t: (0, 0, 0))],
        out_specs=[tab_spec, tab_spec, tab_spec, tab_spec,
                   pl.BlockSpec((N_HEADS, tn), lambda t: (0, t))],
        out_shape=[tab, tab, tab, tab, jax.ShapeDtypeStruct((N_HEADS, n), F32)],
        scratch_shapes=[pltpu.VMEM((2 * N_HEADS, PEER_NKEYS, tn), F32),
                        pltpu.VMEM((PEER_TOPK, tn), F32),
                        pltpu.VMEM((PEER_TOPK, tn), F32),
                        pltpu.VMEM((N_CAND_ROWS, tn), F32),
                        pltpu.VMEM((PEER_TOPK, tn), F32)],
        compiler_params=_cparams("parallel"),
        name="route",
    )(ht, wqt, keys)


EXPERTS_PER_STEP = 1024


def _gelu(x):
    return 0.5 * x * (1.0 + lax.erf(x * (2.0 ** -0.5)))


def _dense_kernel(n_chunks, ht_ref, u_ref, vt_ref, s1_ref, e1_ref, s2_ref, e2_ref, tau_ref, yt_ref,
                  act0_scr, act1_scr, wa0_scr, wa1_scr):
    tn = ht_ref.shape[1]
    g = pl.program_id(0)

    @pl.when(g == 0)
    def _():
        for ref in (act0_scr, act1_scr, wa0_scr, wa1_scr):
            ref[...] = jnp.zeros_like(ref)

    @pl.when(jnp.logical_or(g == 0, (g + 2 * n_chunks - 2) % n_chunks == 0))
    def _():
        yt_ref[...] = jnp.zeros_like(yt_ref)

    n_first = EXPERTS_PER_STEP // PEER_NKEYS
    j_rows = 16
    j_sub = j_rows // 8
    pieces = 4
    j_per_piece = PEER_NKEYS // j_rows // pieces
    a_rows = EXPERTS_PER_STEP // pieces
    c_rows = D_MODEL // pieces

    def piece(act_a, act_b, wa_b, wa_c, jq, carry):
        rows_a = slice(jq * a_rows, (jq + 1) * a_rows)
        rows_c = slice(jq * c_rows, (jq + 1) * c_rows)
        half = tn // 2

        def gate_block(js, lt):
            j0 = (jq * j_per_piece + js) * j_rows
            cols = slice(lt * LANES, (lt + 1) * LANES)
            w = [jnp.zeros((j_sub, 8, LANES), F32) for _ in range(n_first)]
            for h in range(N_HEADS):
                s2q = s2_ref[h, j0:j0 + j_rows, cols].reshape(j_sub, 8, LANES)
                e2q = e2_ref[h, j0:j0 + j_rows, cols].reshape(j_sub, 8, LANES)
                tau = jnp.broadcast_to(tau_ref[h:h + 1, cols], (8, LANES))
                for ii in range(n_first):
                    s1b = jnp.broadcast_to(s1_ref[h, ii:ii + 1, cols], (8, LANES))
                    e1b = jnp.broadcast_to(e1_ref[h, ii:ii + 1, cols], (8, LANES))
                    w[ii] = w[ii] + jnp.where(s1b + s2q >= tau, e1b * e2q, 0.0)
            for ii in range(n_first):
                er = slice(ii * PEER_NKEYS + j0, ii * PEER_NKEYS + j0 + j_rows)
                wa_b[er, cols] = (w[ii].reshape(j_rows, LANES) * act_b[er, cols]).astype(BF16)

        blocks = [(js, lt) for js in range(j_per_piece) for lt in range(tn // LANES)]
        kt = 256
        jobs = []
        for part in range(2):
            cs = slice(part * half, (part + 1) * half)
            for k in range(D_MODEL // kt):
                jobs.append(("a", cs, k))
        for part in range(2):
            cs = slice(part * half, (part + 1) * half)
            for k in range(EXPERTS_PER_STEP // kt):
                jobs.append(("c", cs, k))
        every = len(jobs) // len(blocks)
        for n, (kind, cs, k) in enumerate(jobs):
            ks = slice(k * kt, (k + 1) * kt)
            if kind == "a":
                res = jnp.dot(u_ref[rows_a, ks], ht_ref[ks, cs], preferred_element_type=F32)
                if k == 0:
                    act_a[rows_a, cs] = res
                elif k < D_MODEL // kt - 1:
                    act_a[rows_a, cs] += res
                else:
                    act_a[rows_a, cs] = _gelu(act_a[rows_a, cs] + res)
            else:
                yt_ref[rows_c, cs] += jnp.dot(vt_ref[rows_c, ks], wa_c[ks, cs],
                                              preferred_element_type=F32)
            if n % every == every - 1 and n // every < len(blocks):
                gate_block(*blocks[n // every])
        return carry

    @pl.when(g % 2 == 0)
    def _():
        for jq in range(pieces):
            piece(act0_scr, act1_scr, wa1_scr, wa0_scr, jq, 0)

    @pl.when(g % 2 == 1)
    def _():
        for jq in range(pieces):
            piece(act1_scr, act0_scr, wa0_scr, wa1_scr, jq, 0)


def _dense(ht, u_tab, vt_tab, s1, e1, s2, e2, tau):
    n = ht.shape[1]
    tn = 512
    n_chunks = u_tab.shape[0] // EXPERTS_PER_STEP
    total = (n // tn) * n_chunks
    rows_per_step = EXPERTS_PER_STEP // PEER_NKEYS
    job_a = lambda g: jnp.minimum(g, total - 1)
    job_b = lambda g: jnp.clip(g - 1, 0, total - 1)
    job_c = lambda g: jnp.clip(g - 2, 0, total - 1)
    row_spec = pl.BlockSpec((N_HEADS, rows_per_step, tn),
                            lambda g: (0, job_b(g) % n_chunks, job_b(g) // n_chunks))
    tab_spec = pl.BlockSpec((N_HEADS, PEER_NKEYS, tn), lambda g: (0, 0, job_b(g) // n_chunks))
    return pl.pallas_call(
        functools.partial(_dense_kernel, n_chunks),
        grid=(total + 2,),
        in_specs=[pl.BlockSpec((D_MODEL, tn), lambda g: (0, job_a(g) // n_chunks)),
                  pl.BlockSpec((EXPERTS_PER_STEP, D_MODEL), lambda g: (job_a(g) % n_chunks, 0)),
                  pl.BlockSpec((D_MODEL, EXPERTS_PER_STEP), lambda g: (0, job_c(g) % n_chunks)),
                  row_spec, row_spec, tab_spec, tab_spec,
                  pl.BlockSpec((N_HEADS, tn), lambda g: (0, job_b(g) // n_chunks))],
        out_specs=pl.BlockSpec((D_MODEL, tn), lambda g: (0, job_c(g) // n_chunks)),
        out_shape=jax.ShapeDtypeStruct((D_MODEL, n), F32),
        scratch_shapes=[pltpu.VMEM((EXPERTS_PER_STEP, tn), F32),
                        pltpu.VMEM((EXPERTS_PER_STEP, tn), F32),
                        pltpu.VMEM((EXPERTS_PER_STEP, tn), BF16),
                        pltpu.VMEM((EXPERTS_PER_STEP, tn), BF16)],
        compiler_params=_cparams("arbitrary"),
        name="dense",
    )(ht, u_tab, vt_tab, s1, e1, s2, e2, tau)


def _post2_kernel(alpha, yt_ref, x_ref, g2p_ref, lg_ref, lb_ref, o_ref):
    y = yt_ref[...].T
    o_ref[...] = _layernorm(alpha * x_ref[...] + g2p_ref[0] * y) * lg_ref[...] + lb_ref[...]


def _post2(yt, x1, g2p, ln_g, ln_b, seq, alpha):
    n = x1.shape[0]
    tm = min(512, seq)
    tpb = seq // tm
    return pl.pallas_call(
        functools.partial(_post2_kernel, alpha),
        grid=(n // tm,),
        in_specs=[pl.BlockSpec((D_MODEL, tm), lambda i: (0, i)),
                  pl.BlockSpec((tm, D_MODEL), lambda i: (i, 0)),
                  pl.BlockSpec((1, 1, D_MODEL), lambda i: (i // tpb, 0, 0)),
                  pl.BlockSpec((1, D_MODEL), lambda i: (0, 0)),
                  pl.BlockSpec((1, D_MODEL), lambda i: (0, 0))],
        out_specs=pl.BlockSpec((tm, D_MODEL), lambda i: (i, 0)),
        out_shape=jax.ShapeDtypeStruct((n, D_MODEL), F32),
        compiler_params=_cparams("parallel"),
        name="post2",
    )(yt, x1, g2p, ln_g, ln_b)


def _layer(x2, c, bsz, seq, alpha, w_ada, b_ada, w_in, conv_w, a_log, dt_bias, dn_norm_w, w_pa,
           w_pb, w_o, ln1_g, ln1_b, peer_wq, peer_keys, peer_u, peer_v, ln2_g, ln2_b):
    mod = _ada(c, w_ada, b_ada).reshape(bsz, 6, 1, D_MODEL)
    sh1, sc1, g1, sh2, sc2, g2 = (mod[:, i] for i in range(6))

    o_b, o_a, o_g, o_sb, o_ga = 3072, 3080, 3088, 4112, 7184
    w_main = jnp.concatenate([w_in[:, :o_b], w_in[:, o_sb:o_ga], w_in[:, o_ga:], w_in[:, o_g:o_sb]],
                             axis=1).astype(BF16)
    w_ba = jnp.zeros((D_MODEL, LANES), F32).at[:, :2 * N_HEADS].set(w_in[:, o_b:o_g]).astype(BF16)
    avec = jnp.zeros((1, LANES), F32).at[0, N_HEADS:2 * N_HEADS].set(a_log)
    dtvec = jnp.zeros((1, LANES), F32).at[0, N_HEADS:2 * N_HEADS].set(dt_bias)

    p, p_ba = _inproj(x2, 1.0 + sc1, sh1, w_main, w_ba, seq)
    qkv = _conv(p, conv_w, bsz, seq)
    o_a = _delta(qkv, p, p_ba, avec, dtvec, dn_norm_w.reshape(1, HEAD_DIM), bsz, seq)
    o_b = _sb(p, bsz, seq)
    merged = _merge(o_a, o_b, p, w_pa.astype(BF16), w_pb.astype(BF16))
    x1, ht = _post1(merged, x2, w_o.astype(BF16), 1.0 + g1, ln1_g.reshape(1, D_MODEL),
                    ln1_b.reshape(1, D_MODEL), 1.0 + sc2, sh2, seq, alpha)

    keys = peer_keys.reshape(2 * N_HEADS, PEER_NKEYS, HEAD_DIM).astype(BF16)
    s1, e1, s2, e2, tau = _route(ht, peer_wq.T.astype(BF16), keys)
    yt = _dense(ht, peer_u.astype(BF16), peer_v.T.astype(BF16), s1, e1, s2, e2, tau)
    return _post2(yt, x1, 1.0 + g2, ln2_g.reshape(1, D_MODEL), ln2_b.reshape(1, D_MODEL), seq, alpha)


def kernel(x, c, w_ada, b_ada, w_in, conv_w, a_log, dt_bias, dn_norm_w, w_pa, w_pb, w_o, ln1_g, ln1_b,
           peer_wq, peer_keys, peer_u, peer_v, ln2_g, ln2_b):
    bsz, seq, _ = x.shape
    depth = w_ada.shape[0]
    alpha = (2 * depth) ** 0.25
    x2 = x.reshape(bsz * seq, D_MODEL)
    for l in range(depth):
        x2 = _layer(x2, c, bsz, seq, alpha, w_ada[l], b_ada[l], w_in[l], conv_w[l], a_log[l], dt_bias[l],
                    dn_norm_w[l], w_pa[l], w_pb[l], w_o[l], ln1_g[l], ln1_b[l], peer_wq[l], peer_keys[l],
                    peer_u[l], peer_v[l], ln2_g[l], ln2_b[l])
    return x2.reshape(bsz, seq, D_MODEL)
```

```python
import functools
import math

import jax
import jax.numpy as jnp
from jax import lax
from jax.experimental import pallas as pl
from jax.experimental.pallas import tpu as pltpu

F32 = jnp.float32
BF16 = jnp.bfloat16
HIGHEST = lax.Precision.HIGHEST

LN_EPS = 1e-5
NORM_EPS = 1e-6
D_MODEL = 2048
LANES = 128
CHUNK = 64
N_HEADS = 8
HEAD_DIM = 128
DN_CONV = 4
QKV_W = 3 * N_HEADS * HEAD_DIM
SB_QBLOCK = 128
PEER_NKEYS = 128
PEER_TOPK = 16
EXP_UNDERFLOW = -104.0
VMEM_LIMIT = 56 * 1024 * 1024

COL_DNQKV = 0
COL_SBQKV = 3072
COL_GATE_A = 6144
COL_GATE_B = 8192
COL_DNG = 10240
MAIN_W = 11264


def _cparams(*sem):
    return pltpu.CompilerParams(dimension_semantics=sem, vmem_limit_bytes=VMEM_LIMIT)


def _dot(a, b):
    return jnp.dot(a.astype(BF16), b.astype(BF16), preferred_element_type=F32)


def _dot_nt(a, b):
    return lax.dot_general(a.astype(BF16), b.astype(BF16), (((1,), (1,)), ((), ())),
                           preferred_element_type=F32)


def _silu(x):
    return x * jax.nn.sigmoid(x)


def _softplus(x):
    return jnp.maximum(x, 0.0) + jnp.log1p(jnp.exp(-jnp.abs(x)))


def _layernorm(x):
    mu = jnp.mean(x, axis=-1, keepdims=True)
    xc = x - mu
    var = jnp.mean(xc * xc, axis=-1, keepdims=True)
    return xc * lax.rsqrt(var + LN_EPS)


def _ada_kernel(c_ref, w_ref, b_ref, o_ref):
    s = _silu(c_ref[...])
    o_ref[...] = jnp.dot(s, w_ref[...], preferred_element_type=F32, precision=HIGHEST) + b_ref[...]


def _ada(c, w_ada, b_ada):
    bsz = c.shape[0]
    width = w_ada.shape[1]
    tn = 1024
    cp = jnp.zeros((8, D_MODEL), F32).at[:bsz].set(c)
    out = pl.pallas_call(
        _ada_kernel,
        grid=(width // tn,),
        in_specs=[pl.BlockSpec((8, D_MODEL), lambda j: (0, 0)),
                  pl.BlockSpec((D_MODEL, tn), lambda j: (0, j)),
                  pl.BlockSpec((1, tn), lambda j: (0, j))],
        out_specs=pl.BlockSpec((8, tn), lambda j: (0, j)),
        out_shape=jax.ShapeDtypeStruct((8, width), F32),
        compiler_params=_cparams("parallel"),
        name="ada",
    )(cp, w_ada, b_ada.reshape(1, width))
    return out[:bsz]


def _inproj_kernel(x_ref, scp_ref, sh_ref, w_ref, wba_ref, p_ref, pba_ref, h_scr):
    @pl.when(pl.program_id(1) == 0)
    def _():
        h = (_layernorm(x_ref[...]) * scp_ref[0] + sh_ref[0]).astype(BF16)
        h_scr[...] = h
        pba_ref[...] = jnp.dot(h, wba_ref[...], preferred_element_type=F32)

    p_ref[...] = jnp.dot(h_scr[...], w_ref[...], preferred_element_type=F32).astype(p_ref.dtype)


def _inproj(x2, scp, sh, w_main, w_ba, seq):
    n = x2.shape[0]
    tm = min(1024, seq)
    tn = 1024
    tpb = seq // tm
    return pl.pallas_call(
        _inproj_kernel,
        grid=(n // tm, MAIN_W // tn),
        in_specs=[pl.BlockSpec((tm, D_MODEL), lambda i, j: (i, 0)),
                  pl.BlockSpec((1, 1, D_MODEL), lambda i, j: (i // tpb, 0, 0)),
                  pl.BlockSpec((1, 1, D_MODEL), lambda i, j: (i // tpb, 0, 0)),
                  pl.BlockSpec((D_MODEL, tn), lambda i, j: (0, j)),
                  pl.BlockSpec((D_MODEL, LANES), lambda i, j: (0, 0))],
        out_specs=[pl.BlockSpec((tm, tn), lambda i, j: (i, j)),
                   pl.BlockSpec((tm, LANES), lambda i, j: (i, 0))],
        out_shape=[jax.ShapeDtypeStruct((n, MAIN_W), BF16),
                   jax.ShapeDtypeStruct((n, LANES), F32)],
        scratch_shapes=[pltpu.VMEM((tm, D_MODEL), BF16)],
        compiler_params=_cparams("parallel", "arbitrary"),
        name="inproj",
    )(x2, scp, sh, w_main, w_ba)


def _conv_kernel(x_ref, w_ref, o_ref, prev_scr):
    tt = x_ref.shape[0]

    @pl.when(pl.program_id(1) == 0)
    def _():
        prev_scr[...] = jnp.zeros_like(prev_scr)

    for g in range(QKV_W // LANES):
        cols = slice(g * LANES, (g + 1) * LANES)
        x = x_ref[:, cols].astype(F32)
        xp = jnp.concatenate([prev_scr[:, cols], x], axis=0)
        w = w_ref[:, cols]
        y = w[DN_CONV - 1:DN_CONV] * x
        for i in range(DN_CONV - 1):
            off = 8 - (DN_CONV - 1) + i
            y = y + w[i:i + 1] * xp[off:off + tt]
        prev_scr[:, cols] = x[tt - 8:]
        y = _silu(y)
        if g < 2 * N_HEADS:
            y = y * lax.rsqrt(jnp.sum(y * y, axis=-1, keepdims=True) + NORM_EPS)
        o_ref[:, cols] = y.astype(o_ref.dtype)


def _conv(p, conv_w, bsz, seq):
    tt = min(256, seq)
    tpb = seq // tt
    return pl.pallas_call(
        _conv_kernel,
        grid=(bsz, tpb),
        in_specs=[pl.BlockSpec((tt, QKV_W), lambda b, t: (b * tpb + t, COL_DNQKV // QKV_W)),
                  pl.BlockSpec((DN_CONV, QKV_W), lambda b, t: (0, 0))],
        out_specs=pl.BlockSpec((tt, QKV_W), lambda b, t: (b * tpb + t, 0)),
        out_shape=jax.ShapeDtypeStruct((bsz * seq, QKV_W), BF16),
        scratch_shapes=[pltpu.VMEM((8, QKV_W), F32)],
        compiler_params=_cparams("parallel", "arbitrary"),
        name="conv",
    )(p, conv_w)


def _bdot(a, b):
    return lax.dot_general(a.astype(BF16), b.astype(BF16), (((2,), (1,)), ((0,), (0,))),
                           preferred_element_type=F32)


def _bdot_nt(a, b):
    return lax.dot_general(a.astype(BF16), b.astype(BF16), (((2,), (2,)), ((0,), (0,))),
                           preferred_element_type=F32)


def _bdot_exact(a, b):
    return lax.dot_general(a, b, (((2,), (1,)), ((0,), (0,))), preferred_element_type=F32,
                           precision=HIGHEST)


def _delta_kernel(q_ref, k_ref, v_ref, gate_ref, ba_ref, avec_ref, dtvec_ref, nw_ref, o_ref,
                  s_scr, wq_scr, u_scr, akd_scr, cd_scr, beta_scr, gc_scr):
    tt = q_ref.shape[0]
    nc = tt // CHUNK
    h = pl.program_id(2)

    @pl.when(jnp.logical_and(pl.program_id(1) == 0, h == 0))
    def _():
        s_scr[...] = jnp.zeros_like(s_scr)

    r = lax.broadcasted_iota(jnp.int32, (nc, CHUNK, CHUNK), 1)
    c = lax.broadcasted_iota(jnp.int32, (nc, CHUNK, CHUNK), 2)
    incl = r >= c
    strict = r > c
    eye = (r == c).astype(F32)

    @pl.when(h == 0)
    def _():
        ba = ba_ref[...]
        beta_scr[...] = jax.nn.sigmoid(ba)
        g_all = -jnp.exp(avec_ref[...]) * _softplus(ba + dtvec_ref[...])
        gc_all = _bdot_exact(incl.astype(F32), g_all.reshape(nc, CHUNK, LANES))
        gc_scr[...] = gc_all.reshape(tt, LANES)

    lane = lax.broadcasted_iota(jnp.int32, (tt, LANES), 1)
    b_col = jnp.sum(jnp.where(lane == h, beta_scr[...], 0.0), axis=1, keepdims=True)
    gc_col = jnp.sum(jnp.where(lane == h + N_HEADS, gc_scr[...], 0.0), axis=1, keepdims=True)
    beta = jnp.broadcast_to(b_col, (tt, LANES)).reshape(nc, CHUNK, LANES)
    gc = jnp.broadcast_to(gc_col, (tt, LANES)).reshape(nc, CHUNK, LANES)
    gc_rows = jnp.stack([gc[ci].T[:CHUNK] for ci in range(nc)])

    q = q_ref[...].astype(F32).reshape(nc, CHUNK, HEAD_DIM) * (HEAD_DIM ** -0.5)
    k = k_ref[...].astype(F32).reshape(nc, CHUNK, HEAD_DIM)
    v = v_ref[...].astype(F32).reshape(nc, CHUNK, HEAD_DIM)
    decay = jnp.where(incl, jnp.exp(gc[:, :, :CHUNK] - gc_rows), 0.0)
    egc = jnp.exp(gc)
    gl = gc[:, CHUNK - 1:CHUNK, :]
    kb = k * beta
    p = jnp.where(strict, -(_bdot_nt(kb, k) * decay), 0.0)
    tm = eye + p
    for _ in range(CHUNK.bit_length() - 2):
        p = _bdot(p, p)
        tm = tm + _bdot(tm, p)
    wq_scr[h, :, :CHUNK, :] = _bdot(tm, kb * egc).astype(BF16)
    wq_scr[h, :, CHUNK:, :] = (q * egc).astype(BF16)
    u_scr[h] = _bdot(tm, v * beta)
    akd_scr[h, :, :CHUNK, :] = (_bdot_nt(q, k) * decay).astype(BF16)
    kd = k * jnp.exp(gl - gc)
    for ci in range(nc):
        akd_scr[h, ci, CHUNK:, :] = kd[ci].T.astype(BF16)
    cd_scr[h] = jnp.exp(gl)

    @pl.when(h == N_HEADS - 1)
    def _():
        def chunk(ci, carry):
            rows = pl.ds(pl.multiple_of(ci * CHUNK, CHUNK), CHUNK)
            heads = range(N_HEADS)
            ws = [jnp.dot(wq_scr[hh, ci], s_scr[hh].astype(BF16), preferred_element_type=F32)
                  for hh in heads]
            v_new = [(u_scr[hh, ci] - ws[hh][:CHUNK]).astype(BF16) for hh in heads]
            av = [jnp.dot(akd_scr[hh, ci], v_new[hh], preferred_element_type=F32) for hh in heads]
            for hh in heads:
                s_scr[hh] = s_scr[hh] * cd_scr[hh, ci] + av[hh][CHUNK:]
            for hh in heads:
                cols = slice(hh * HEAD_DIM, (hh + 1) * HEAD_DIM)
                o = ws[hh][CHUNK:] + av[hh][:CHUNK]
                on = o * lax.rsqrt(jnp.mean(o * o, axis=-1, keepdims=True) + NORM_EPS) * nw_ref[...]
                o_ref[rows, cols] = (on * _silu(gate_ref[rows, cols].astype(F32))).astype(o_ref.dtype)
            return carry

        lax.fori_loop(0, nc, chunk, 0)


def _delta(qkv, p, p_ba, avec, dtvec, nw, bsz, seq):
    tt = min(1024, seq)
    nc = tt // CHUNK
    tpb = seq // tt
    hw = N_HEADS * HEAD_DIM
    row = lambda b, t, h: b * tpb + t
    return pl.pallas_call(
        _delta_kernel,
        grid=(bsz, tpb, N_HEADS),
        in_specs=[pl.BlockSpec((tt, HEAD_DIM), lambda b, t, h: (row(b, t, h), h)),
                  pl.BlockSpec((tt, HEAD_DIM), lambda b, t, h: (row(b, t, h), N_HEADS + h)),
                  pl.BlockSpec((tt, HEAD_DIM), lambda b, t, h: (row(b, t, h), 2 * N_HEADS + h)),
                  pl.BlockSpec((tt, hw), lambda b, t, h: (row(b, t, h), COL_DNG // hw)),
                  pl.BlockSpec((tt, LANES), lambda b, t, h: (row(b, t, h), 0)),
                  pl.BlockSpec((1, LANES), lambda b, t, h: (0, 0)),
                  pl.BlockSpec((1, LANES), lambda b, t, h: (0, 0)),
                  pl.BlockSpec((1, HEAD_DIM), lambda b, t, h: (0, 0))],
        out_specs=pl.BlockSpec((tt, hw), lambda b, t, h: (row(b, t, h), 0)),
        out_shape=jax.ShapeDtypeStruct((bsz * seq, hw), BF16),
        scratch_shapes=[pltpu.VMEM((N_HEADS, HEAD_DIM, HEAD_DIM), F32),
                        pltpu.VMEM((N_HEADS, nc, 2 * CHUNK, HEAD_DIM), BF16),
                        pltpu.VMEM((N_HEADS, nc, CHUNK, HEAD_DIM), F32),
                        pltpu.VMEM((N_HEADS, nc, CHUNK + HEAD_DIM, CHUNK), BF16),
                        pltpu.VMEM((N_HEADS, nc, 1, HEAD_DIM), F32),
                        pltpu.VMEM((tt, LANES), F32),
                        pltpu.VMEM((tt, LANES), F32)],
        compiler_params=_cparams("parallel", "arbitrary", "arbitrary"),
        name="delta",
    )(qkv, qkv, qkv, p, p_ba, avec, dtvec, nw)


def _sb_kernel(q_ref, k_ref, v_ref, o_ref, acc_scr, run_scr):
    blk = SB_QBLOCK
    n_chain = q_ref.shape[0] // blk
    chains = range(n_chain)
    first = pl.program_id(2) * n_chain
    scale = HEAD_DIM ** -0.5
    row = lax.broadcasted_iota(jnp.int32, (blk, blk), 0)
    col = lax.broadcasted_iota(jnp.int32, (blk, blk), 1)
    below = jnp.where(col < row, 1.0, 0.0)
    ones = jnp.ones((blk, blk), F32)
    r2 = lax.broadcasted_iota(jnp.int32, (blk, 2 * blk), 0)
    c2 = lax.broadcasted_iota(jnp.int32, (blk, 2 * blk), 1)
    cum_m = jnp.where((r2 > c2) | (c2 >= blk), 1.0, 0.0).astype(BF16)
    acc_scr[...] = jnp.zeros_like(acc_scr)
    run_scr[...] = jnp.zeros_like(run_scr)

    def body(carry):
        dist, _ = carry
        kbs = [first + c - dist for c in chains]
        ks = [pl.ds(pl.multiple_of(jnp.maximum(kb, 0) * blk, blk), blk) for kb in kbs]
        z = [_dot_nt(q_ref[c * blk:(c + 1) * blk, :], k_ref[ks[c], :]) * scale for c in chains]
        sp = [jnp.maximum(z[c], 0.0) + jnp.log(1.0 + jnp.exp(-jnp.abs(z[c]))) for c in chains]
        on_diag = jnp.where(dist > 0, ones, below)
        valid = [on_diag * jnp.where(kbs[c] >= 0, 1.0, 0.0) for c in chains]
        ls = [-sp[c] * valid[c] for c in chains]
        hi = [ls[c].astype(BF16) for c in chains]
        lo = [(ls[c] - hi[c].astype(F32)).astype(BF16) for c in chains]
        ct = [jnp.dot(hi[c], cum_m, preferred_element_type=F32)
              + jnp.dot(lo[c], cum_m, preferred_element_type=F32) for c in chains]
        live = jnp.full((blk, blk), -jnp.inf, F32)
        for c in chains:
            run = run_scr[c]
            a = jnp.exp((z[c] - sp[c]) + ct[c][:, :blk] + run) * valid[c]
            acc_scr[c] += _dot(a, v_ref[ks[c], :])
            run = run + ct[c][:, blk:]
            run_scr[c] = run
            live = jnp.maximum(live, jnp.where(kbs[c] >= 1, run, -jnp.inf))
        return dist + 1, jnp.max(live) >= EXP_UNDERFLOW

    lax.while_loop(lambda carry: carry[1], body, (jnp.int32(0), True))
    for c in chains:
        o_ref[c * blk:(c + 1) * blk, :] = acc_scr[c].astype(o_ref.dtype)


def _sb(p, bsz, seq):
    n_chain = min(8, seq // SB_QBLOCK)
    tq = n_chain * SB_QBLOCK
    nq = seq // tq
    cq = COL_SBQKV // HEAD_DIM
    return pl.pallas_call(
        _sb_kernel,
        grid=(bsz, N_HEADS, nq),
        in_specs=[pl.BlockSpec((tq, HEAD_DIM), lambda b, h, i: (b * nq + i, cq + h)),
                  pl.BlockSpec((seq, HEAD_DIM), lambda b, h, i: (b, cq + N_HEADS + h)),
                  pl.BlockSpec((seq, HEAD_DIM), lambda b, h, i: (b, cq + 2 * N_HEADS + h))],
        out_specs=pl.BlockSpec((tq, HEAD_DIM), lambda b, h, i: (b * nq + i, h)),
        out_shape=jax.ShapeDtypeStruct((bsz * seq, N_HEADS * HEAD_DIM), BF16),
        scratch_shapes=[pltpu.VMEM((n_chain, SB_QBLOCK, HEAD_DIM), F32),
                        pltpu.VMEM((n_chain, SB_QBLOCK, SB_QBLOCK), F32)],
        compiler_params=_cparams("parallel", "parallel", "arbitrary"),
        name="sb",
    )(p, p, p)


def _merge_kernel(oa_ref, ob_ref, ga_ref, gb_ref, wpa_ref, wpb_ref, o_ref):
    ya = jnp.dot(oa_ref[...], wpa_ref[...], preferred_element_type=F32)
    yb = jnp.dot(ob_ref[...], wpb_ref[...], preferred_element_type=F32)
    m = (jax.nn.sigmoid(ga_ref[...].astype(F32)) * ya
         + jax.nn.sigmoid(gb_ref[...].astype(F32)) * yb)
    o_ref[...] = m.astype(o_ref.dtype)


def _merge(o_a, o_b, p, w_pa, w_pb):
    n = o_a.shape[0]
    tm = 512
    hw = N_HEADS * HEAD_DIM
    return pl.pallas_call(
        _merge_kernel,
        grid=(n // tm,),
        in_specs=[pl.BlockSpec((tm, hw), lambda i: (i, 0)),
                  pl.BlockSpec((tm, hw), lambda i: (i, 0)),
                  pl.BlockSpec((tm, D_MODEL), lambda i: (i, COL_GATE_A // D_MODEL)),
                  pl.BlockSpec((tm, D_MODEL), lambda i: (i, COL_GATE_B // D_MODEL)),
                  pl.BlockSpec((hw, D_MODEL), lambda i: (0, 0)),
                  pl.BlockSpec((hw, D_MODEL), lambda i: (0, 0))],
        out_specs=pl.BlockSpec((tm, D_MODEL), lambda i: (i, 0)),
        out_shape=jax.ShapeDtypeStruct((n, D_MODEL), BF16),
        compiler_params=_cparams("parallel"),
        name="merge",
    )(o_a, o_b, p, p, w_pa, w_pb)


def _post1_kernel(alpha, m_ref, x_ref, wo_ref, g1p_ref, lg_ref, lb_ref, scp_ref, sh_ref,
                  x1_ref, ht_ref):
    y = jnp.dot(m_ref[...], wo_ref[...], preferred_element_type=F32)
    x1 = _layernorm(alpha * x_ref[...] + g1p_ref[0] * y) * lg_ref[...] + lb_ref[...]
    x1_ref[...] = x1
    h2 = _layernorm(x1) * scp_ref[0] + sh_ref[0]
    ht_ref[...] = h2.T.astype(ht_ref.dtype)


def _post1(merged, x2, w_o, g1p, ln_g, ln_b, scp2, sh2, seq, alpha):
    n = x2.shape[0]
    tm = min(512, seq)
    tpb = seq // tm
    bvec = lambda i: (i // tpb, 0, 0)
    return pl.pallas_call(
        functools.partial(_post1_kernel, alpha),
        grid=(n // tm,),
        in_specs=[pl.BlockSpec((tm, D_MODEL), lambda i: (i, 0)),
                  pl.BlockSpec((tm, D_MODEL), lambda i: (i, 0)),
                  pl.BlockSpec((D_MODEL, D_MODEL), lambda i: (0, 0)),
                  pl.BlockSpec((1, 1, D_MODEL), bvec),
                  pl.BlockSpec((1, D_MODEL), lambda i: (0, 0)),
                  pl.BlockSpec((1, D_MODEL), lambda i: (0, 0)),
                  pl.BlockSpec((1, 1, D_MODEL), bvec),
                  pl.BlockSpec((1, 1, D_MODEL), bvec)],
        out_specs=[pl.BlockSpec((tm, D_MODEL), lambda i: (i, 0)),
                   pl.BlockSpec((D_MODEL, tm), lambda i: (0, i))],
        out_shape=[jax.ShapeDtypeStruct((n, D_MODEL), F32),
                   jax.ShapeDtypeStruct((D_MODEL, n), BF16)],
        compiler_params=_cparams("parallel"),
        name="post1",
    )(merged, x2, w_o, g1p, ln_g, ln_b, scp2, sh2)


def _extract_top(lists, count, out_scrs, one_at_a_time):
    lists = list(lists)
    rids = [lax.broadcasted_iota(jnp.int32, s.shape, 0).astype(F32) for s in lists]
    for r in range(count):
        for i, s in enumerate(lists):
            m = jnp.max(s, axis=0, keepdims=True)
            if one_at_a_time:
                first = jnp.min(jnp.where(s == m, rids[i], float(s.shape[0])), axis=0, keepdims=True)
                lists[i] = jnp.where(rids[i] == first, -jnp.inf, s)
            else:
                lists[i] = jnp.where(s == m, -jnp.inf, s)
            out_scrs[i][r:r + 1, :] = m
    return lists


def _most_removed(lists):
    counts = [jnp.sum(jnp.where(s == -jnp.inf, 1.0, 0.0), axis=0, keepdims=True) for s in lists]
    return jnp.max(functools.reduce(jnp.maximum, counts))


N_CAND_ROWS = 80
N_CAND_PAD = 30


def _route_kernel(ht_ref, wqt_ref, keys_ref, s1_ref, e1_ref, s2_ref, e2_ref, tau_ref,
                  st_scr, v1_scr, v2_scr, cand_scr, top_scr):
    qt = jnp.dot(wqt_ref[...], ht_ref[...], preferred_element_type=F32)
    for hp in range(2 * N_HEADS):
        st_scr[hp] = jnp.dot(keys_ref[hp], qt[hp * HEAD_DIM:(hp + 1) * HEAD_DIM].astype(BF16),
                             preferred_element_type=F32)
    rid8 = lax.broadcasted_iota(jnp.int32, (8, ht_ref.shape[1]), 0)
    neg = -jnp.inf

    def head(h, carry):
        s1 = st_scr[2 * h]
        s2 = st_scr[2 * h + 1]
        left = _extract_top([s1, s2], PEER_TOPK, [v1_scr, v2_scr], False)

        @pl.when(_most_removed(left) > PEER_TOPK)
        def _():
            _extract_top([s1, s2], PEER_TOPK, [v1_scr, v2_scr], True)

        cand_scr[0:16, :] = v1_scr[0:1, :] + v2_scr[0:16, :]
        lo8 = v2_scr[0:8, :]
        for a in range(1, 8):
            nb = PEER_TOPK // (a + 1)
            cand_scr[8 + 8 * a:16 + 8 * a, :] = jnp.where(rid8 < nb, v1_scr[a:a + 1, :] + lo8, neg)
        cand_scr[72:80, :] = v1_scr[8:16, :] + v2_scr[0:1, :]
        cand = cand_scr[...]
        left = _extract_top([cand], PEER_TOPK, [top_scr], False)

        @pl.when(_most_removed(left) > PEER_TOPK + N_CAND_PAD)
        def _():
            _extract_top([cand], PEER_TOPK, [top_scr], True)

        tau = top_scr[PEER_TOPK - 1:PEER_TOPK, :]
        top = top_scr[0:1, :]
        z = jnp.sum(jnp.where(cand >= tau, jnp.exp(cand - top), 0.0), axis=0, keepdims=True)
        s1_ref[h] = s1
        s2_ref[h] = s2
        e1_ref[h] = jnp.exp(s1 - v1_scr[0:1, :])
        e2_ref[h] = jnp.exp(s2 - v2_scr[0:1, :]) * (1.0 / z)
        tau_ref[pl.ds(h, 1), :] = tau
        return carry

    lax.fori_loop(0, N_HEADS, head, 0)


def _route(ht, wqt, keys):
    n = ht.shape[1]
    tn = 2 * LANES
    tab = jax.ShapeDtypeStruct((N_HEADS, PEER_NKEYS, n), F32)
    tab_spec = pl.BlockSpec((N_HEADS, PEER_NKEYS, tn), lambda t: (0, 0, t))
    return pl.pallas_call(
        _route_kernel,
        grid=(n // tn,),
        in_specs=[pl.BlockSpec((D_MODEL, tn), lambda t: (0, t)),
                  pl.BlockSpec((D_MODEL, D_MODEL), lambda t: (0, 0)),
                  pl.BlockSpec((2 * N_HEADS, PEER_NKEYS, HEAD_DIM), lambda t: (0, 0, 0))],
        out_specs=[tab_spec, tab_spec, tab_spec, tab_spec,
                   pl.BlockSpec((N_HEADS, tn), lambda t: (0, t))],
        out_shape=[tab, tab, tab, tab, jax.ShapeDtypeStruct((N_HEADS, n), F32)],
        scratch_shapes=[pltpu.VMEM((2 * N_HEADS, PEER_NKEYS, tn), F32),
                        pltpu.VMEM((PEER_TOPK, tn), F32),
                        pltpu.VMEM((PEER_TOPK, tn), F32),
                        pltpu.VMEM((N_CAND_ROWS, tn), F32),
                        pltpu.VMEM((PEER_TOPK, tn), F32)],
        compiler_params=_cparams("parallel"),
        name="route",
    )(ht, wqt, keys)


EXPERTS_PER_STEP = 1024


def _gelu(x):
    return 0.5 * x * (1.0 + lax.erf(x * (2.0 ** -0.5)))


def _dense_kernel(n_chunks, ht_ref, u_ref, vt_ref, s1_ref, e1_ref, s2_ref, e2_ref, tau_ref, yt_ref,
                  act0_scr, act1_scr, wa0_scr, wa1_scr):
    tn = ht_ref.shape[1]
    g = pl.program_id(0)

    @pl.when(g == 0)
    def _():
        for ref in (act0_scr, act1_scr, wa0_scr, wa1_scr):
            ref[...] = jnp.zeros_like(ref)

    @pl.when(jnp.logical_or(g == 0, (g + 2 * n_chunks - 2) % n_chunks == 0))
    def _():
        yt_ref[...] = jnp.zeros_like(yt_ref)

    n_first = EXPERTS_PER_STEP // PEER_NKEYS
    j_rows = 16
    j_sub = j_rows // 8
    pieces = 4
    j_per_piece = PEER_NKEYS // j_rows // pieces
    a_rows = EXPERTS_PER_STEP // pieces
    c_rows = D_MODEL // pieces

    def piece(act_a, act_b, wa_b, wa_c, jq, carry):
        rows_a = slice(jq * a_rows, (jq + 1) * a_rows)
        rows_c = slice(jq * c_rows, (jq + 1) * c_rows)
        half = tn // 2

        def gate_block(js, lt):
            j0 = (jq * j_per_piece + js) * j_rows
            cols = slice(lt * LANES, (lt + 1) * LANES)
            w = [jnp.zeros((j_sub, 8, LANES), F32) for _ in range(n_first)]
            for h in range(N_HEADS):
                s2q = s2_ref[h, j0:j0 + j_rows, cols].reshape(j_sub, 8, LANES)
                e2q = e2_ref[h, j0:j0 + j_rows, cols].reshape(j_sub, 8, LANES)
                tau = jnp.broadcast_to(tau_ref[h:h + 1, cols], (8, LANES))
                for ii in range(n_first):
                    s1b = jnp.broadcast_to(s1_ref[h, ii:ii + 1, cols], (8, LANES))
                    e1b = jnp.broadcast_to(e1_ref[h, ii:ii + 1, cols], (8, LANES))
                    w[ii] = w[ii] + jnp.where(s1b + s2q >= tau, e1b * e2q, 0.0)
            for ii in range(n_first):
                er = slice(ii * PEER_NKEYS + j0, ii * PEER_NKEYS + j0 + j_rows)
                wa_b[er, cols] = (w[ii].reshape(j_rows, LANES) * act_b[er, cols]).astype(BF16)

        blocks = [(js, lt) for js in range(j_per_piece) for lt in range(tn // LANES)]
        kt = 256
        jobs = []
        for part in range(2):
            cs = slice(part * half, (part + 1) * half)
            for k in range(D_MODEL // kt):
                jobs.append(("a", cs, k))
        for part in range(2):
            cs = slice(part * half, (part + 1) * half)
            for k in range(EXPERTS_PER_STEP // kt):
                jobs.append(("c", cs, k))
        every = len(jobs) // len(blocks)
        for n, (kind, cs, k) in enumerate(jobs):
            ks = slice(k * kt, (k + 1) * kt)
            if kind == "a":
                res = jnp.dot(u_ref[rows_a, ks], ht_ref[ks, cs], preferred_element_type=F32)
                if k == 0:
                    act_a[rows_a, cs] = res
                elif k < D_MODEL // kt - 1:
                    act_a[rows_a, cs] += res
                else:
                    act_a[rows_a, cs] = _gelu(act_a[rows_a, cs] + res)
            else:
                yt_ref[rows_c, cs] += jnp.dot(vt_ref[0, rows_c, ks], wa_c[ks, cs],
                                              preferred_element_type=F32)
            if n % every == every - 1 and n // every < len(blocks):
                gate_block(*blocks[n // every])
        return carry

    @pl.when(g % 2 == 0)
    def _():
        for jq in range(pieces):
            piece(act0_scr, act1_scr, wa1_scr, wa0_scr, jq, 0)

    @pl.when(g % 2 == 1)
    def _():
        for jq in range(pieces):
            piece(act1_scr, act0_scr, wa0_scr, wa1_scr, jq, 0)


def _dense(ht, u_tab, vt_tab, s1, e1, s2, e2, tau):
    n = ht.shape[1]
    tn = 512
    n_chunks = vt_tab.shape[0]
    total = (n // tn) * n_chunks
    rows_per_step = EXPERTS_PER_STEP // PEER_NKEYS
    job_a = lambda g: jnp.minimum(g, total - 1)
    job_b = lambda g: jnp.clip(g - 1, 0, total - 1)
    job_c = lambda g: jnp.clip(g - 2, 0, total - 1)
    row_spec = pl.BlockSpec((N_HEADS, rows_per_step, tn),
                            lambda g: (0, job_b(g) % n_chunks, job_b(g) // n_chunks))
    tab_spec = pl.BlockSpec((N_HEADS, PEER_NKEYS, tn), lambda g: (0, 0, job_b(g) // n_chunks))
    return pl.pallas_call(
        functools.partial(_dense_kernel, n_chunks),
        grid=(total + 2,),
        in_specs=[pl.BlockSpec((D_MODEL, tn), lambda g: (0, job_a(g) // n_chunks)),
                  pl.BlockSpec((EXPERTS_PER_STEP, D_MODEL), lambda g: (job_a(g) % n_chunks, 0)),
                  pl.BlockSpec((1, D_MODEL, EXPERTS_PER_STEP), lambda g: (job_c(g) % n_chunks, 0, 0)),
                  row_spec, row_spec, tab_spec, tab_spec,
                  pl.BlockSpec((N_HEADS, tn), lambda g: (0, job_b(g) // n_chunks))],
        out_specs=pl.BlockSpec((D_MODEL, tn), lambda g: (0, job_c(g) // n_chunks)),
        out_shape=jax.ShapeDtypeStruct((D_MODEL, n), F32),
        scratch_shapes=[pltpu.VMEM((EXPERTS_PER_STEP, tn), F32),
                        pltpu.VMEM((EXPERTS_PER_STEP, tn), F32),
                        pltpu.VMEM((EXPERTS_PER_STEP, tn), BF16),
                        pltpu.VMEM((EXPERTS_PER_STEP, tn), BF16)],
        compiler_params=_cparams("arbitrary"),
        name="dense",
    )(ht, u_tab, vt_tab, s1, e1, s2, e2, tau)


def _post2_kernel(alpha, yt_ref, x_ref, g2p_ref, lg_ref, lb_ref, o_ref):
    y = yt_ref[...].T
    o_ref[...] = _layernorm(alpha * x_ref[...] + g2p_ref[0] * y) * lg_ref[...] + lb_ref[...]


def _post2(yt, x1, g2p, ln_g, ln_b, seq, alpha):
    n = x1.shape[0]
    tm = min(512, seq)
    tpb = seq // tm
    return pl.pallas_call(
        functools.partial(_post2_kernel, alpha),
        grid=(n // tm,),
        in_specs=[pl.BlockSpec((D_MODEL, tm), lambda i: (0, i)),
                  pl.BlockSpec((tm, D_MODEL), lambda i: (i, 0)),
                  pl.BlockSpec((1, 1, D_MODEL), lambda i: (i // tpb, 0, 0)),
                  pl.BlockSpec((1, D_MODEL), lambda i: (0, 0)),
                  pl.BlockSpec((1, D_MODEL), lambda i: (0, 0))],
        out_specs=pl.BlockSpec((tm, D_MODEL), lambda i: (i, 0)),
        out_shape=jax.ShapeDtypeStruct((n, D_MODEL), F32),
        compiler_params=_cparams("parallel"),
        name="post2",
    )(yt, x1, g2p, ln_g, ln_b)


def _layer(x2, c, bsz, seq, alpha, w_ada, b_ada, w_in, conv_w, a_log, dt_bias, dn_norm_w, w_pa,
           w_pb, w_o, ln1_g, ln1_b, peer_wq, peer_keys, peer_u, peer_v, ln2_g, ln2_b):
    mod = _ada(c, w_ada, b_ada).reshape(bsz, 6, 1, D_MODEL)
    sh1, sc1, g1, sh2, sc2, g2 = (mod[:, i] for i in range(6))

    o_b, o_a, o_g, o_sb, o_ga = 3072, 3080, 3088, 4112, 7184
    w_main = jnp.concatenate([w_in[:, :o_b], w_in[:, o_sb:o_ga], w_in[:, o_ga:], w_in[:, o_g:o_sb]],
                             axis=1).astype(BF16)
    w_ba = jnp.zeros((D_MODEL, LANES), F32).at[:, :2 * N_HEADS].set(w_in[:, o_b:o_g]).astype(BF16)
    avec = jnp.zeros((1, LANES), F32).at[0, N_HEADS:2 * N_HEADS].set(a_log)
    dtvec = jnp.zeros((1, LANES), F32).at[0, N_HEADS:2 * N_HEADS].set(dt_bias)

    p, p_ba = _inproj(x2, 1.0 + sc1, sh1, w_main, w_ba, seq)
    qkv = _conv(p, conv_w, bsz, seq)
    o_a = _delta(qkv, p, p_ba, avec, dtvec, dn_norm_w.reshape(1, HEAD_DIM), bsz, seq)
    o_b = _sb(p, bsz, seq)
    merged = _merge(o_a, o_b, p, w_pa.astype(BF16), w_pb.astype(BF16))
    x1, ht = _post1(merged, x2, w_o.astype(BF16), 1.0 + g1, ln1_g.reshape(1, D_MODEL),
                    ln1_b.reshape(1, D_MODEL), 1.0 + sc2, sh2, seq, alpha)

    keys = peer_keys.reshape(2 * N_HEADS, PEER_NKEYS, HEAD_DIM).astype(BF16)
    s1, e1, s2, e2, tau = _route(ht, peer_wq.T.astype(BF16), keys)
    vt = peer_v.reshape(-1, EXPERTS_PER_STEP, D_MODEL).transpose(0, 2, 1).astype(BF16)
    yt = _dense(ht, peer_u.astype(BF16), vt, s1, e1, s2, e2, tau)
    return _post2(yt, x1, 1.0 + g2, ln2_g.reshape(1, D_MODEL), ln2_b.reshape(1, D_MODEL), seq, alpha)


def kernel(x, c, w_ada, b_ada, w_in, conv_w, a_log, dt_bias, dn_norm_w, w_pa, w_pb, w_o, ln1_g, ln1_b,
           peer_wq, peer_keys, peer_u, peer_v, ln2_g, ln2_b):
    bsz, seq, _ = x.shape
    depth = w_ada.shape[0]
    alpha = (2 * depth) ** 0.25
    x2 = x.reshape(bsz * seq, D_MODEL)
    for l in range(depth):
        x2 = _layer(x2, c, bsz, seq, alpha, w_ada[l], b_ada[l], w_in[l], conv_w[l], a_log[l], dt_bias[l],
                    dn_norm_w[l], w_pa[l], w_pb[l], w_o[l], ln1_g[l], ln1_b[l], peer_wq[l], peer_keys[l],
                    peer_u[l], peer_v[l], ln2_g[l], ln2_b[l])
    return x2.reshape(bsz, seq, D_MODEL)
```
